```python
import math
import jax, jax.numpy as jnp
from jax import lax
import numpy as np

D_MODEL = 2048
BATCH = 8
SEQ = 4096
DEPTH = 2

DIFF_HEADS = 4
DIFF_DQK = 128
DIFF_DV = 2 * DIFF_DQK
ROT_DIM = DIFF_DQK // 4
ROPE_THETA = 500000.0
Q_BLOCK = 128
GLA_HEADS = 4
GLA_DK = 128
GLA_DV = 256
GLA_GATE_RANK = 16
GLA_TAU = 16.0
GLA_CHUNK = 64
D_FF = 5632
CONV_WIDTH = 3
MAX_POS_OFFSET = 1024
NORM_EPS = 1e-6

IN_SIZES = (
    DIFF_HEADS * 2 * DIFF_DQK,
    DIFF_HEADS * 2 * DIFF_DQK,
    DIFF_HEADS * DIFF_DV,
    GLA_HEADS * GLA_DK,
    GLA_HEADS * GLA_DK,
    GLA_HEADS * GLA_DV,
    GLA_HEADS * GLA_DV,
    GLA_GATE_RANK,
    GLA_GATE_RANK,
    D_MODEL,
    D_MODEL,
)
IN_COLS = sum(IN_SIZES)

kernel_name = "hybrid_diffattn_gla_convffn_adaln"


def rms_norm(x, g):
    xf = x.astype(jnp.float32)
    y = xf * lax.rsqrt(jnp.mean(xf * xf, axis=-1, keepdims=True) + NORM_EPS)
    return (y * g).astype(x.dtype)


def rotary_tables(positions):
    inv_freq = ROPE_THETA ** (-jnp.arange(0, ROT_DIM, 2, dtype=jnp.float32) / ROT_DIM)
    ang = positions.astype(jnp.float32)[..., None] * inv_freq
    return jnp.cos(ang)[:, :, None, None, :], jnp.sin(ang)[:, :, None, None, :]


def apply_partial_rotary(t, cos, sin):
    half = ROT_DIM // 2
    x1, x2, rest = t[..., :half], t[..., half:ROT_DIM], t[..., ROT_DIM:]
    rot = jnp.concatenate([x1 * cos - x2 * sin, x2 * cos + x1 * sin], axis=-1)
    return jnp.concatenate([rot.astype(t.dtype), rest], axis=-1)


def diff_attention(q, k, v, lam):
    B, S, H, _, dqk = q.shape
    nb = S // Q_BLOCK
    qb = jnp.moveaxis(q.reshape(B, nb, Q_BLOCK, H, 2, dqk), 1, 0)
    scale = dqk ** -0.5

    def block(qblk):
        s = jnp.einsum('bqhmd,bkhmd->bhmqk', qblk, k).astype(jnp.float32) * scale
        p = jax.nn.softmax(s, axis=-1)
        w = p[:, :, 0] - lam * p[:, :, 1]
        return jnp.einsum('bhqk,bkhe->bqhe', w.astype(v.dtype), v)

    out = lax.map(block, qb)
    return jnp.moveaxis(out, 0, 1).reshape(B, S, H, v.shape[-1])


def gla_scan(q, k, v, log_a):
    B, S, H, dk = q.shape
    dv = v.shape[-1]
    nc = S // GLA_CHUNK

    def chunks(t):
        return t.reshape(B, nc, GLA_CHUNK, H, t.shape[-1]).transpose(1, 0, 3, 2, 4)

    lower = jnp.tril(jnp.ones((GLA_CHUNK, GLA_CHUNK), dtype=bool))[:, :, None]

    def step(state, inp):
        qc, kc, vc, ac = inp
        b = jnp.cumsum(ac, axis=2)
        b_last = b[:, :, -1:, :]
        o_inter = jnp.einsum('bhcd,bhde->bhce', qc * jnp.exp(b), state)
        rel = b[:, :, :, None, :] - b[:, :, None, :, :]
        decay = jnp.exp(jnp.where(lower, rel, -jnp.inf))
        scores = jnp.einsum('bhid,bhjd,bhijd->bhij', qc, kc, decay)
        o_intra = jnp.einsum('bhij,bhje->bhie', scores, vc)
        state = (state * jnp.exp(b_last[:, :, 0, :])[..., None]
                 + jnp.einsum('bhjd,bhje->bhde', kc * jnp.exp(b_last - b), vc))
        return state, o_inter + o_intra

    state0 = jnp.zeros((B, H, dk, dv), jnp.float32)
    _, out = lax.scan(step, state0, (chunks(q), chunks(k), chunks(v), chunks(log_a)))
    return out.transpose(1, 0, 3, 2, 4).reshape(B, S, H, dv).astype(v.dtype)


def token_mixer(h, cos, sin, layer, w_in, lq1, lk1, lq2, lk2, subln_g,
                w2f, bf, w2b, bb, gla_g, w_bd, w_bg, w_o):
    B, S, _ = h.shape
    split_points = [int(p) for p in np.cumsum(IN_SIZES)[:-1]]
    (dq, dk, dv, gq, gk, gv, gr, glf, glb, gate_a, gate_b) = jnp.split(h @ w_in, split_points, axis=-1)

    q = apply_partial_rotary(dq.reshape(B, S, DIFF_HEADS, 2, DIFF_DQK), cos, sin)
    k = apply_partial_rotary(dk.reshape(B, S, DIFF_HEADS, 2, DIFF_DQK), cos, sin)
    v = dv.reshape(B, S, DIFF_HEADS, DIFF_DV)
    lam_init = 0.8 - 0.6 * math.exp(-0.3 * layer)
    lam = (jnp.exp(jnp.sum(lq1 * lk1).astype(jnp.float32))
           - jnp.exp(jnp.sum(lq2 * lk2).astype(jnp.float32)) + lam_init)
    o_d = rms_norm(diff_attention(q, k, v, lam), subln_g) * (1.0 - lam_init)
    y_diff = o_d.reshape(B, S, -1) @ w_bd

    q_g = gq.reshape(B, S, GLA_HEADS, GLA_DK) * (GLA_DK ** -0.5)
    k_g = gk.reshape(B, S, GLA_HEADS, GLA_DK)
    v_g = gv.reshape(B, S, GLA_HEADS, GLA_DV)
    log_af = (jax.nn.log_sigmoid((glf @ w2f + bf).astype(jnp.float32)) / GLA_TAU).reshape(B, S, GLA_HEADS, GLA_DK)
    log_ab = (jax.nn.log_sigmoid((glb @ w2b + bb).astype(jnp.float32)) / GLA_TAU).reshape(B, S, GLA_HEADS, GLA_DK)
    o_f = gla_scan(q_g, k_g, v_g, log_af)
    flip = lambda t: jnp.flip(t, axis=1)
    o_b = flip(gla_scan(flip(q_g), flip(k_g), flip(v_g), flip(log_ab)))
    o_g = rms_norm(o_f + o_b, gla_g) * jax.nn.silu(gr.reshape(B, S, GLA_HEADS, GLA_DV))
    y_gla = o_g.reshape(B, S, -1) @ w_bg

    merged = jax.nn.sigmoid(gate_a) * y_diff + jax.nn.sigmoid(gate_b) * y_gla
    return merged @ w_o


def conv_ffn(h, w_gate, w_up, conv_w, conv_b, w_down):
    g = h @ w_gate
    g = lax.conv_general_dilated(
        g, conv_w[:, None, :], window_strides=(1,),
        padding=((CONV_WIDTH // 2, CONV_WIDTH // 2),),
        dimension_numbers=('NWC', 'WIO', 'NWC'),
        feature_group_count=g.shape[-1]) + conv_b
    return (jax.nn.silu(g) * (h @ w_up)) @ w_down


def setup_inputs(seed: int = 0) -> dict:
    key = jax.random.key(seed)
    ks = jax.random.split(key, 27)
    f32 = jnp.float32
    L, D, F = DEPTH, D_MODEL, D_FF

    def nrm(k, shape, scale):
        return jax.random.normal(k, shape, f32) * scale

    x = nrm(ks[0], (BATCH, SEQ, D), 1.0)
    c = nrm(ks[1], (BATCH, D), 1.0)
    positions = (jnp.arange(SEQ, dtype=jnp.int32)[None, :]
                 + jax.random.randint(ks[2], (BATCH, 1), 0, MAX_POS_OFFSET, dtype=jnp.int32))
    return {
        "x": x,
        "c": c,
        "positions": positions,
        "w_ada": nrm(ks[3], (L, D, 6 * D), D ** -0.5),
        "b_ada": nrm(ks[4], (L, 6 * D), 0.02),
        "norm_mix_g": 1.0 + nrm(ks[5], (L, D), 0.02),
        "w_in": nrm(ks[6], (L, D, IN_COLS), D ** -0.5),
        "lambda_q1": nrm(ks[7], (L, DIFF_DQK), 0.1),
        "lambda_k1": nrm(ks[8], (L, DIFF_DQK), 0.1),
        "lambda_q2": nrm(ks[9], (L, DIFF_DQK), 0.1),
        "lambda_k2": nrm(ks[10], (L, DIFF_DQK), 0.1),
        "diff_subln_g": 1.0 + nrm(ks[11], (L, DIFF_DV), 0.02),
        "gla_w2_fwd": nrm(ks[12], (L, GLA_GATE_RANK, GLA_HEADS * GLA_DK), GLA_GATE_RANK ** -0.5),
        "gla_b_fwd": nrm(ks[13], (L, GLA_HEADS * GLA_DK), 0.02),
        "gla_w2_bwd": nrm(ks[14], (L, GLA_GATE_RANK, GLA_HEADS * GLA_DK), GLA_GATE_RANK ** -0.5),
        "gla_b_bwd": nrm(ks[15], (L, GLA_HEADS * GLA_DK), 0.02),
        "gla_norm_g": 1.0 + nrm(ks[16], (L, GLA_DV), 0.02),
        "w_branch_diff": nrm(ks[17], (L, DIFF_HEADS * DIFF_DV, D), (DIFF_HEADS * DIFF_DV) ** -0.5),
        "w_branch_gla": nrm(ks[18], (L, GLA_HEADS * GLA_DV, D), (GLA_HEADS * GLA_DV) ** -0.5),
        "w_out": nrm(ks[19], (L, D, D), D ** -0.5),
        "norm_ffn_g": 1.0 + nrm(ks[20], (L, D), 0.02),
        "w_gate": nrm(ks[21], (L, D, F), D ** -0.5),
        "w_up": nrm(ks[22], (L, D, F), D ** -0.5),
        "conv_w": nrm(ks[23], (L, CONV_WIDTH, F), CONV_WIDTH ** -0.5),
        "conv_b": nrm(ks[24], (L, F), 0.02),
        "w_down": nrm(ks[25], (L, F, D), F ** -0.5),
        "final_norm_g": 1.0 + nrm(ks[26], (D,), 0.02),
    }


def reference(x, c, positions, w_ada, b_ada, norm_mix_g, w_in, lambda_q1, lambda_k1,
              lambda_q2, lambda_k2, diff_subln_g, gla_w2_fwd, gla_b_fwd, gla_w2_bwd,
              gla_b_bwd, gla_norm_g, w_branch_diff, w_branch_gla, w_out, norm_ffn_g,
              w_gate, w_up, conv_w, conv_b, w_down, final_norm_g):
    cos, sin = rotary_tables(positions)
    c_act = jax.nn.silu(c)
    for l in range(DEPTH):
        mod = c_act @ w_ada[l] + b_ada[l]
        sh1, sc1, gt1, sh2, sc2, gt2 = [m[:, None, :] for m in jnp.split(mod, 6, axis=-1)]
        h = rms_norm(x, norm_mix_g[l]) * (1.0 + sc1) + sh1
        x = x + gt1 * token_mixer(
            h, cos, sin, l, w_in[l], lambda_q1[l], lambda_k1[l], lambda_q2[l], lambda_k2[l],
            diff_subln_g[l], gla_w2_fwd[l], gla_b_fwd[l], gla_w2_bwd[l], gla_b_bwd[l],
            gla_norm_g[l], w_branch_diff[l], w_branch_gla[l], w_out[l])
        h = rms_norm(x, norm_ffn_g[l]) * (1.0 + sc2) + sh2
        x = x + gt2 * conv_ffn(h, w_gate[l], w_up[l], conv_w[l], conv_b[l], w_down[l])
    return rms_norm(x, final_norm_g)
```

```python
import functools
import math

import jax
import jax.numpy as jnp
from jax import lax
from jax.experimental import pallas as pl
from jax.experimental.pallas import tpu as pltpu

F32 = jnp.float32
BF16 = jnp.bfloat16

DIFF_HEADS = 4
DIFF_DQK = 128
DIFF_DV = 2 * DIFF_DQK
ROT_DIM = DIFF_DQK // 4
ROPE_THETA = 500000.0
GLA_HEADS = 4
GLA_DK = 128
GLA_DV = 256
GLA_GATE_RANK = 16
GLA_TAU = 16.0
GLA_CHUNK = 64
CONV_WIDTH = 3
NORM_EPS = 1e-6
LOG2E = 1.4426950408889634

LANES = 128
BF16_SUBLANES = 16
VMEM_BUDGET_BYTES = 58 * 1024 * 1024

ADA_TN = 1536
INPROJ_TM = 1024
INPROJ_TN = 512
ATTN_TQ = 256
GLA_ROWS = 256
POST_TM = 256
FFN_TM = 512
FFN_TN = 512
FFN_HALO = BF16_SUBLANES


def _cparams(sem, vmem_bytes):
    return pltpu.CompilerParams(dimension_semantics=sem, vmem_limit_bytes=min(int(vmem_bytes), VMEM_BUDGET_BYTES))


def _nt_dot(a, b):
    return lax.dot_general(a, b, (((1,), (1,)), ((), ())), preferred_element_type=F32)


def _tn_dot(a, b):
    return lax.dot_general(a, b, (((0,), (0,)), ((), ())), preferred_element_type=F32)


def _rms_norm(x, g):
    return x * lax.rsqrt(jnp.mean(x * x, axis=-1, keepdims=True) + NORM_EPS) * g


def _ada_kernel(c_ref, w_ref, b_ref, o_ref):
    c = c_ref[...]
    ca = (c * jax.nn.sigmoid(c)).astype(BF16)
    o_ref[0] = jnp.dot(ca, w_ref[0].astype(BF16), preferred_element_type=F32) + b_ref[0]


def _ada_call(c, w_ada, b_ada):
    L, D, N = w_ada.shape
    B = c.shape[0]
    tn = ADA_TN
    vmem = 2 * (D * tn * 4) + D * tn * 2 + 4 * B * (D + 2 * tn) * 4 + (2 << 20)
    return pl.pallas_call(
        _ada_kernel,
        grid=(L, N // tn),
        in_specs=[
            pl.BlockSpec((B, D), lambda l, j: (0, 0)),
            pl.BlockSpec((1, D, tn), lambda l, j: (l, 0, j)),
            pl.BlockSpec((1, 1, tn), lambda l, j: (l, 0, j)),
        ],
        out_specs=pl.BlockSpec((1, B, tn), lambda l, j: (l, 0, j)),
        out_shape=jax.ShapeDtypeStruct((L, B, N), F32),
        compiler_params=_cparams(("arbitrary", "arbitrary"), vmem),
        name="ada_mod",
    )(c, w_ada, b_ada.reshape(L, 1, N))


_QK_BLOCKS = (0, 4)
_DV_BLOCKS = (4, 6)
_GQK_BLOCKS = (6, 8)
_GV_BLOCKS = (8, 10)
_GR_BLOCKS = (10, 12)
_GA_BLOCKS = (12, 16)
_GB_BLOCKS = (16, 20)


def _inproj_kernel(x_ref, mod_ref, g_ref, pos_ref, freq_ref, sgn_ref, w_ref, wlr_ref,
                   qk_ref, dv_ref, gqk_ref, gv_ref, gr_ref, ga_ref, gb_ref, lr_ref,
                   h_scr, cos_scr, sin_scr):
    j = pl.program_id(1)

    @pl.when(j == 0)
    def _prologue():
        y = _rms_norm(x_ref[...], g_ref[...])
        h = (y * (1.0 + mod_ref[0, 1:2, :]) + mod_ref[0, 0:1, :]).astype(BF16)
        h_scr[...] = h
        lr_ref[...] = jnp.dot(h, wlr_ref[...], preferred_element_type=F32)
        ang = pos_ref[...].astype(F32) * freq_ref[...]
        cos_scr[...] = jnp.cos(ang)
        sin_scr[...] = jnp.sin(ang) * sgn_ref[...]

    acc = jnp.dot(h_scr[...], w_ref[...], preferred_element_type=F32)
    tn = acc.shape[1]

    def in_blocks(rng):
        return jnp.logical_and(j >= rng[0], j < rng[1])

    @pl.when(in_blocks(_QK_BLOCKS))
    def _qk():
        lane = lax.broadcasted_iota(jnp.int32, (acc.shape[0], LANES), 1)
        half = ROT_DIM // 2
        qscale = jnp.where(j < 2, (DIFF_DQK ** -0.5) * LOG2E, 1.0).astype(F32)
        for gi in range(tn // LANES):
            t = acc[:, gi * LANES:(gi + 1) * LANES]
            partner = jnp.where(lane < half, pltpu.roll(t, LANES - half, 1), pltpu.roll(t, half, 1))
            r = jnp.where(lane < ROT_DIM, t * cos_scr[...] + partner * sin_scr[...], t)
            qk_ref[:, gi * LANES:(gi + 1) * LANES] = (r * qscale).astype(BF16)

    @pl.when(in_blocks(_DV_BLOCKS))
    def _dv():
        dv_ref[...] = acc.astype(BF16)

    @pl.when(in_blocks(_GQK_BLOCKS))
    def _gqk():
        gqk_ref[...] = acc * jnp.where(j == _GQK_BLOCKS[0], GLA_DK ** -0.5, 1.0).astype(F32)

    @pl.when(in_blocks(_GV_BLOCKS))
    def _gv():
        gv_ref[...] = acc.astype(BF16)

    @pl.when(in_blocks(_GR_BLOCKS))
    def _gr():
        gr_ref[...] = (acc * jax.nn.sigmoid(acc)).astype(BF16)

    @pl.when(in_blocks(_GA_BLOCKS))
    def _ga():
        ga_ref[...] = jax.nn.sigmoid(acc).astype(BF16)

    @pl.when(in_blocks(_GB_BLOCKS))
    def _gb():
        gb_ref[...] = jax.nn.sigmoid(acc).astype(BF16)


def _inproj_call(x2, mod_l, norm_g, pos2, freq, sgn, w_main, w_lr, seq):
    M, D = x2.shape
    tm, tn = INPROJ_TM, INPROJ_TN
    nblk = w_main.shape[1] // tn
    assert nblk == _GB_BLOCKS[1] and seq % tm == 0
    bpb = seq // tm

    def seg(rng):
        return lambda i, j: (i, jnp.clip(j - rng[0], 0, rng[1] - rng[0] - 1))

    def nseg(rng):
        return (rng[1] - rng[0]) * tn

    out_shape = (
        jax.ShapeDtypeStruct((M, nseg(_QK_BLOCKS)), BF16),
        jax.ShapeDtypeStruct((M, nseg(_DV_BLOCKS)), BF16),
        jax.ShapeDtypeStruct((M, nseg(_GQK_BLOCKS)), F32),
        jax.ShapeDtypeStruct((M, nseg(_GV_BLOCKS)), BF16),
        jax.ShapeDtypeStruct((M, nseg(_GR_BLOCKS)), BF16),
        jax.ShapeDtypeStruct((M, nseg(_GA_BLOCKS)), BF16),
        jax.ShapeDtypeStruct((M, nseg(_GB_BLOCKS)), BF16),
        jax.ShapeDtypeStruct((M, LANES), F32),
    )
    out_specs = (
        pl.BlockSpec((tm, tn), seg(_QK_BLOCKS)),
        pl.BlockSpec((tm, tn), seg(_DV_BLOCKS)),
        pl.BlockSpec((tm, tn), seg(_GQK_BLOCKS)),
        pl.BlockSpec((tm, tn), seg(_GV_BLOCKS)),
        pl.BlockSpec((tm, tn), seg(_GR_BLOCKS)),
        pl.BlockSpec((tm, tn), seg(_GA_BLOCKS)),
        pl.BlockSpec((tm, tn), seg(_GB_BLOCKS)),
        pl.BlockSpec((tm, LANES), lambda i, j: (i, 0)),
    )
    vmem = (2 * tm * D * 4 + tm * D * 2 + 2 * D * tn * 2 + 2 * D * LANES * 2
            + 2 * tm * tn * (2 * 6 + 4) + 2 * tm * LANES * 4 * 2 + 2 * tm * LANES * 4
            + 4 * tm * tn * 4 + (4 << 20))
    return pl.pallas_call(
        _inproj_kernel,
        grid=(M // tm, nblk),
        in_specs=[
            pl.BlockSpec((tm, D), lambda i, j: (i, 0)),
            pl.BlockSpec((1, 6, D), lambda i, j: (i // bpb, 0, 0)),
            pl.BlockSpec((1, D), lambda i, j: (0, 0)),
            pl.BlockSpec((tm, 1), lambda i, j: (i, 0)),
            pl.BlockSpec((1, LANES), lambda i, j: (0, 0)),
            pl.BlockSpec((1, LANES), lambda i, j: (0, 0)),
            pl.BlockSpec((D, tn), lambda i, j: (0, j)),
            pl.BlockSpec((D, LANES), lambda i, j: (0, 0)),
        ],
        out_specs=out_specs,
        out_shape=out_shape,
        scratch_shapes=[
            pltpu.VMEM((tm, D), BF16),
            pltpu.VMEM((tm, LANES), F32),
            pltpu.VMEM((tm, LANES), F32),
        ],
        compiler_params=_cparams(("arbitrary", "arbitrary"), vmem),
        name="in_proj",
    )(x2, mod_l, norm_g, pos2, freq, sgn, w_main, w_lr)


def _attn_kernel(lam_init, q_ref, k_ref, v_ref, lqk_ref, g_ref, o_ref):
    q = q_ref[...]
    k = k_ref[...]
    v = v_ref[...]
    lqk = lqk_ref[...]
    lam = (jnp.exp(jnp.sum(lqk[0:1] * lqk[1:2], axis=-1, keepdims=True))
           - jnp.exp(jnp.sum(lqk[2:3] * lqk[3:4], axis=-1, keepdims=True)) + lam_init)

    def one_map(mi):
        sl = slice(mi * DIFF_DQK, (mi + 1) * DIFF_DQK)
        s = _nt_dot(q[:, sl], k[:, sl])
        m = jnp.max(s, axis=-1, keepdims=True)
        p = jnp.exp2(s - m)
        l = jnp.sum(p, axis=-1, keepdims=True)
        return jnp.dot(p.astype(BF16), v, preferred_element_type=F32) / l

    o = one_map(0) - lam * one_map(1)
    o_ref[...] = (_rms_norm(o, g_ref[...]) * (1.0 - lam_init)).astype(BF16)


def _attn_call(qk, dv, lqk, subln_g, lam_init, batch, seq):
    M = qk.shape[0]
    tq = ATTN_TQ
    nq = seq // tq
    H = DIFF_HEADS
    w = 2 * DIFF_DQK
    vmem = 2 * (tq * w * 2) * 2 + 2 * 2 * (seq * w * 2) + 5 * tq * seq * 4 + (4 << 20)
    return pl.pallas_call(
        functools.partial(_attn_kernel, lam_init),
        grid=(batch, H, nq),
        in_specs=[
            pl.BlockSpec((tq, w), lambda b, h, i: (b * nq + i, h)),
            pl.BlockSpec((seq, w), lambda b, h, i: (b, H + h)),
            pl.BlockSpec((seq, DIFF_DV), lambda b, h, i: (b, h)),
            pl.BlockSpec((4, DIFF_DQK), lambda b, h, i: (0, 0)),
            pl.BlockSpec((1, DIFF_DV), lambda b, h, i: (0, 0)),
        ],
        out_specs=pl.BlockSpec((tq, DIFF_DV), lambda b, h, i: (b * nq + i, h)),
        out_shape=jax.ShapeDtypeStruct((M, H * DIFF_DV), BF16),
        compiler_params=_cparams(("arbitrary", "arbitrary", "arbitrary"), vmem),
        name="diff_attn",
    )(qk, qk, dv, lqk, subln_g)


def _gla_kernel(q_ref, k_ref, v_ref, grs_ref, lr_ref, w2f_ref, w2b_ref, bf_ref, bb_ref, g_ref, o_ref,
                qt_scr, kt_scr, kh_scr, d_scr, acc_scr):
    S = q_ref.shape[0]
    C = GLA_CHUNK
    R = GLA_ROWS
    nc = S // C
    shift = int(math.log2(C))

    rr = lax.broadcasted_iota(jnp.int32, (R, R), 0)
    cc = lax.broadcasted_iota(jnp.int32, (R, R), 1)
    same = (rr >> shift) == (cc >> shift)
    ones_bd = jnp.where(same, 1.0, 0.0).astype(BF16)
    r1 = lax.broadcasted_iota(jnp.int32, (C, C), 0)
    c1 = lax.broadcasted_iota(jnp.int32, (C, C), 1)

    def split_dot(mat, a):
        hi = a.astype(BF16)
        lo = (a - hi.astype(F32)).astype(BF16)
        return jnp.dot(mat, hi, preferred_element_type=F32) + jnp.dot(mat, lo, preferred_element_type=F32)

    def direction(rev):
        w2 = (w2b_ref if rev else w2f_ref)[...]
        bias = (bb_ref if rev else bf_ref)[...]
        cum = jnp.where(jnp.logical_and(same, (cc >= rr) if rev else (cc <= rr)), 1.0, 0.0).astype(BF16)

        def pre(i, carry):
            rows = pl.ds(pl.multiple_of(i * R, R), R)
            xg = jnp.dot(lr_ref[rows, :].astype(BF16), w2, preferred_element_type=F32) + bias
            log_a = (jnp.minimum(xg, 0.0) - jnp.log1p(jnp.exp(-jnp.abs(xg)))) * (1.0 / GLA_TAU)
            b = split_dot(cum, log_a)
            tot = split_dot(ones_bd, log_a)
            q = q_ref[rows, :]
            k = k_ref[rows, :]
            qt_scr[rows, :] = (q * jnp.exp(b)).astype(BF16)
            kt_scr[rows, :] = (k * jnp.exp(-b)).astype(BF16)
            kh_scr[rows, :] = (k * jnp.exp(tot - b)).astype(BF16)
            d_scr[rows, :] = jnp.exp(tot)
            return carry

        lax.fori_loop(0, S // R, pre, 0)

        mask = (c1 >= r1) if rev else (c1 <= r1)

        def step(i, st):
            t = (nc - 1 - i) if rev else i
            rows = pl.ds(pl.multiple_of(t * C, C), C)
            qt = qt_scr[rows, :]
            v = v_ref[rows, :]
            sc = jnp.where(mask, _nt_dot(qt, kt_scr[rows, :]), 0.0).astype(BF16)
            o = _nt_dot(qt, st.astype(BF16)) + jnp.dot(sc, v, preferred_element_type=F32)
            if rev:
                acc_scr[rows, :] += o
            else:
                acc_scr[rows, :] = o
            d = d_scr[pl.ds(pl.multiple_of(t * C, C), 1), :]
            return st * d + _tn_dot(v, kh_scr[rows, :])

        lax.fori_loop(0, nc, step, jnp.zeros((GLA_DV, GLA_DK), F32), unroll=2)

    direction(False)
    direction(True)

    def fin(i, carry):
        rows = pl.ds(pl.multiple_of(i * R, R), R)
        y = _rms_norm(acc_scr[rows, :], g_ref[...])
        o_ref[rows, :] = (y * grs_ref[rows, :].astype(F32)).astype(BF16)
        return carry

    lax.fori_loop(0, S // R, fin, 0)


def _gla_call(gqk, gv, grs, lr, w2f_pad, w2b_pad, bf, bb, gla_g, batch, seq):
    M = gqk.shape[0]
    H = GLA_HEADS
    vmem = (2 * (2 * seq * GLA_DK * 4 + 2 * seq * GLA_DV * 2 + seq * LANES * 4 + seq * GLA_DV * 2)
            + 3 * seq * GLA_DK * 2 + seq * GLA_DK * 4 + seq * GLA_DV * 4 + (8 << 20))
    return pl.pallas_call(
        _gla_kernel,
        grid=(batch, H),
        in_specs=[
            pl.BlockSpec((seq, GLA_DK), lambda b, h: (b, h)),
            pl.BlockSpec((seq, GLA_DK), lambda b, h: (b, H + h)),
            pl.BlockSpec((seq, GLA_DV), lambda b, h: (b, h)),
            pl.BlockSpec((seq, GLA_DV), lambda b, h: (b, h)),
            pl.BlockSpec((seq, LANES), lambda b, h: (b, 0)),
            pl.BlockSpec((LANES, GLA_DK), lambda b, h: (0, h)),
            pl.BlockSpec((LANES, GLA_DK), lambda b, h: (0, h)),
            pl.BlockSpec((1, GLA_DK), lambda b, h: (0, h)),
            pl.BlockSpec((1, GLA_DK), lambda b, h: (0, h)),
            pl.BlockSpec((1, GLA_DV), lambda b, h: (0, 0)),
        ],
        out_specs=pl.BlockSpec((seq, GLA_DV), lambda b, h: (b, h)),
        out_shape=jax.ShapeDtypeStruct((M, H * GLA_DV), BF16),
        scratch_shapes=[
            pltpu.VMEM((seq, GLA_DK), BF16),
            pltpu.VMEM((seq, GLA_DK), BF16),
            pltpu.VMEM((seq, GLA_DK), BF16),
            pltpu.VMEM((seq, GLA_DK), F32),
            pltpu.VMEM((seq, GLA_DV), F32),
        ],
        compiler_params=_cparams(("arbitrary", "arbitrary"), vmem),
        name="gla",
    )(gqk, gqk, gv, grs, lr, w2f_pad, w2b_pad, bf, bb, gla_g)


def _post_kernel(od_ref, og_ref, ga_ref, gb_ref, x_ref, mod_ref, wbd_ref, wbg_ref, wo_ref, xo_ref):
    yd = jnp.dot(od_ref[...], wbd_ref[...], preferred_element_type=F32)
    yg = jnp.dot(og_ref[...], wbg_ref[...], preferred_element_type=F32)
    merged = (ga_ref[...].astype(F32) * yd + gb_ref[...].astype(F32) * yg).astype(BF16)
    out = jnp.dot(merged, wo_ref[...], preferred_element_type=F32)
    xo_ref[...] = x_ref[...] + mod_ref[0, 2:3, :] * out


def _post_call(od, og, ga, gb, x2, mod_l, wbd, wbg, wo, seq):
    M, D = x2.shape
    tm = POST_TM
    bpb = seq // tm
    kd, kg = od.shape[1], og.shape[1]
    const = dict(pipeline_mode=pl.Buffered(1))
    vmem = ((kd + kg + D) * D * 2 + 2 * tm * (kd + kg) * 2 + 2 * 2 * tm * D * 2 + 2 * 2 * tm * D * 4
            + 4 * tm * D * 4 + (4 << 20))
    return pl.pallas_call(
        _post_kernel,
        grid=(M // tm,),
        in_specs=[
            pl.BlockSpec((tm, kd), lambda i: (i, 0)),
            pl.BlockSpec((tm, kg), lambda i: (i, 0)),
            pl.BlockSpec((tm, D), lambda i: (i, 0)),
            pl.BlockSpec((tm, D), lambda i: (i, 0)),
            pl.BlockSpec((tm, D), lambda i: (i, 0)),
            pl.BlockSpec((1, 6, D), lambda i: (i // bpb, 0, 0)),
            pl.BlockSpec((kd, D), lambda i: (0, 0), **const),
            pl.BlockSpec((kg, D), lambda i: (0, 0), **const),
            pl.BlockSpec((D, D), lambda i: (0, 0), **const),
        ],
        out_specs=pl.BlockSpec((tm, D), lambda i: (i, 0)),
        out_shape=jax.ShapeDtypeStruct((M, D), F32),
        compiler_params=_cparams(("arbitrary",), vmem),
        name="mix_out",
    )(od, og, ga, gb, x2, mod_l, wbd, wbg, wo)


def _ffn_kernel(final_norm, bpb, x_ref, xp_ref, xn_ref, mod_ref, g_ref, wg_ref, wu_ref, cw_ref, cb_ref, wd_ref, fg_ref,
              o_ref, h_scr, g_scr, acc_scr):
    i = pl.program_id(0)
    j = pl.program_id(1)
    nj = pl.num_programs(1)
    tm = x_ref.shape[0]
    hl = FFN_HALO

    def modulated(xv):
        return _rms_norm(xv, g_ref[...]) * (1.0 + mod_ref[0, 4:5, :]) + mod_ref[0, 3:4, :]

    @pl.when(j == 0)
    def _prologue():
        in_seq_prev = (i % bpb) != 0
        in_seq_next = (i % bpb) != (bpb - 1)
        h_scr[0:hl, :] = jnp.where(in_seq_prev, modulated(xp_ref[...]), 0.0).astype(BF16)
        h_scr[hl:hl + tm, :] = modulated(x_ref[...]).astype(BF16)
        h_scr[hl + tm:hl + tm + hl, :] = jnp.where(in_seq_next, modulated(xn_ref[...]), 0.0).astype(BF16)

    g_scr[...] = jnp.dot(h_scr[...], wg_ref[...], preferred_element_type=F32)
    u = jnp.dot(h_scr[hl:hl + tm, :], wu_ref[...], preferred_element_type=F32)
    cw = cw_ref[...]
    gc = (cw[0:1] * g_scr[pl.ds(hl - 1, tm), :] + cw[1:2] * g_scr[pl.ds(hl, tm), :]
          + cw[2:3] * g_scr[pl.ds(hl + 1, tm), :] + cb_ref[...])
    act = (gc * jax.nn.sigmoid(gc) * u).astype(BF16)
    part = jnp.dot(act, wd_ref[...], preferred_element_type=F32)

    @pl.when(j == 0)
    def _first():
        acc_scr[...] = part

    @pl.when(j > 0)
    def _rest():
        acc_scr[...] += part

    @pl.when(j == nj - 1)
    def _epilogue():
        xo = x_ref[...] + mod_ref[0, 5:6, :] * acc_scr[...]
        if final_norm:
            xo = _rms_norm(xo, fg_ref[...])
        o_ref[...] = xo


def _ffn_call(x2, mod_l, norm_g, wg, wu, conv_w, conv_b, wd, final_g, final_norm, seq):
    M, D = x2.shape
    Fdim = wg.shape[1]
    tm, tn, hl = FFN_TM, FFN_TN, FFN_HALO
    assert seq % tm == 0 and Fdim % tn == 0 and tm % hl == 0
    bpb = seq // tm
    hpb = tm // hl
    nh = M // hl
    vmem = (2 * tm * D * 4 * 2 + 2 * 2 * hl * D * 4 + 2 * 3 * D * tn * 2 + (tm + 2 * hl) * D * 2
            + (tm + 2 * hl) * tn * 4 + tm * D * 4 + 6 * tm * tn * 4 + tm * D * 4 + (4 << 20))
    return pl.pallas_call(
        functools.partial(_ffn_kernel, final_norm, bpb),
        grid=(M // tm, Fdim // tn),
        in_specs=[
            pl.BlockSpec((tm, D), lambda i, j: (i, 0)),
            pl.BlockSpec((hl, D), lambda i, j: (jnp.maximum(i * hpb - 1, 0), 0)),
            pl.BlockSpec((hl, D), lambda i, j: (jnp.minimum((i + 1) * hpb, nh - 1), 0)),
            pl.BlockSpec((1, 6, D), lambda i, j: (i // bpb, 0, 0)),
            pl.BlockSpec((1, D), lambda i, j: (0, 0)),
            pl.BlockSpec((D, tn), lambda i, j: (0, j)),
            pl.BlockSpec((D, tn), lambda i, j: (0, j)),
            pl.BlockSpec((CONV_WIDTH, tn), lambda i, j: (0, j)),
            pl.BlockSpec((1, tn), lambda i, j: (0, j)),
            pl.BlockSpec((tn, D), lambda i, j: (j, 0)),
            pl.BlockSpec((1, D), lambda i, j: (0, 0)),
        ],
        out_specs=pl.BlockSpec((tm, D), lambda i, j: (i, 0)),
        out_shape=jax.ShapeDtypeStruct((M, D), F32),
        scratch_shapes=[
            pltpu.VMEM((tm + 2 * hl, D), BF16),
            pltpu.VMEM((tm + 2 * hl, tn), F32),
            pltpu.VMEM((tm, D), F32),
        ],
        compiler_params=_cparams(("arbitrary", "arbitrary"), vmem),
        name="conv_ffn",
    )(x2, x2, x2, mod_l, norm_g, wg, wu, conv_w, conv_b, wd, final_g)


def kernel(x, c, positions, w_ada, b_ada, norm_mix_g, w_in, lambda_q1, lambda_k1, lambda_q2, lambda_k2,
           diff_subln_g, gla_w2_fwd, gla_b_fwd, gla_w2_bwd, gla_b_bwd, gla_norm_g, w_branch_diff,
           w_branch_gla, w_out, norm_ffn_g, w_gate, w_up, conv_w, conv_b, w_down, final_norm_g):
    B, S, D = x.shape
    L = w_ada.shape[0]
    M = B * S
    x2 = x.reshape(M, D)
    pos2 = positions.reshape(M, 1).astype(jnp.int32)

    half = ROT_DIM // 2
    inv_freq = ROPE_THETA ** (-jnp.arange(0, ROT_DIM, 2, dtype=F32) / ROT_DIM)
    freq = jnp.zeros((1, LANES), F32).at[0, :half].set(inv_freq).at[0, half:ROT_DIM].set(inv_freq)
    sgn = jnp.zeros((1, LANES), F32).at[0, :half].set(-1.0).at[0, half:ROT_DIM].set(1.0)

    mod = _ada_call(c, w_ada, b_ada).reshape(L, B, 6, D)

    sizes = (DIFF_HEADS * 2 * DIFF_DQK, DIFF_HEADS * 2 * DIFF_DQK, DIFF_HEADS * DIFF_DV,
             GLA_HEADS * GLA_DK, GLA_HEADS * GLA_DK, GLA_HEADS * GLA_DV, GLA_HEADS * GLA_DV,
             GLA_GATE_RANK, GLA_GATE_RANK, D, D)
    offs = [0]
    for s_ in sizes:
        offs.append(offs[-1] + s_)

    for l in range(L):
        wl = w_in[l]
        cols = [wl[:, offs[k]:offs[k + 1]] for k in range(len(sizes))]
        w_main = jnp.concatenate(cols[0:7] + cols[9:11], axis=1).astype(BF16)
        w_lr = jnp.zeros((D, LANES), F32).at[:, :GLA_GATE_RANK].set(cols[7])
        w_lr = w_lr.at[:, GLA_GATE_RANK:2 * GLA_GATE_RANK].set(cols[8]).astype(BF16)
        nk = GLA_HEADS * GLA_DK
        w2f_pad = jnp.zeros((LANES, nk), F32).at[:GLA_GATE_RANK].set(gla_w2_fwd[l]).astype(BF16)
        w2b_pad = jnp.zeros((LANES, nk), F32).at[GLA_GATE_RANK:2 * GLA_GATE_RANK].set(gla_w2_bwd[l]).astype(BF16)
        lam_init = 0.8 - 0.6 * math.exp(-0.3 * l)
        lqk = jnp.stack([lambda_q1[l], lambda_k1[l], lambda_q2[l], lambda_k2[l]])

        qk, dv, gqk, gv, grs, ga, gb, lr = _inproj_call(
            x2, mod[l], norm_mix_g[l].reshape(1, D), pos2, freq, sgn, w_main, w_lr, S)
        od = _attn_call(qk, dv, lqk, diff_subln_g[l].reshape(1, DIFF_DV), lam_init, B, S)
        og = _gla_call(gqk, gv, grs, lr, w2f_pad, w2b_pad, gla_b_fwd[l].reshape(1, nk),
                       gla_b_bwd[l].reshape(1, nk), gla_norm_g[l].reshape(1, GLA_DV), B, S)
        x2 = _post_call(od, og, ga, gb, x2, mod[l], w_branch_diff[l].astype(BF16),
                        w_branch_gla[l].astype(BF16), w_out[l].astype(BF16), S)
        x2 = _ffn_call(x2, mod[l], norm_ffn_g[l].reshape(1, D), w_gate[l].astype(BF16), w_up[l].astype(BF16),
                       conv_w[l], conv_b[l].reshape(1, -1), w_down[l].astype(BF16),
                       final_norm_g.reshape(1, D), l == L - 1, S)
    return x2.reshape(B, S, D)
```

```python
import functools
import math

import jax
import jax.numpy as jnp
from jax import lax
from jax.experimental import pallas as pl
from jax.experimental.pallas import tpu as pltpu

F32 = jnp.float32
BF16 = jnp.bfloat16

DIFF_HEADS = 4
DIFF_DQK = 128
DIFF_DV = 2 * DIFF_DQK
ROT_DIM = DIFF_DQK // 4
ROPE_THETA = 500000.0
GLA_HEADS = 4
GLA_DK = 128
GLA_DV = 256
GLA_GATE_RANK = 16
GLA_TAU = 16.0
GLA_CHUNK = 64
CONV_WIDTH = 3
NORM_EPS = 1e-6
LOG2E = 1.4426950408889634

LANES = 128
SUBLANES = 8
BF16_SUBLANES = 16
VMEM_BUDGET_BYTES = 58 * 1024 * 1024

ADA_TN = 1536
INPROJ_TM = 1024
INPROJ_TN = 1024
INPROJ_SUB = 512
ATTN_TQ = 512
ATTN_KC = 1024
GLA_ROWS = 256
GLA_UNROLL = 4
POST_TM = 256
FFN_TM = 512
FFN_TN = 512
FFN_NC = 512
FFN_HALO = BF16_SUBLANES


def _cparams(sem, vmem_bytes):
    return pltpu.CompilerParams(dimension_semantics=sem, vmem_limit_bytes=min(int(vmem_bytes), VMEM_BUDGET_BYTES))


def _nt_dot(a, b):
    return lax.dot_general(a, b, (((1,), (1,)), ((), ())), preferred_element_type=F32)


def _tn_dot(a, b):
    return lax.dot_general(a, b, (((0,), (0,)), ((), ())), preferred_element_type=F32)


def _rms_norm(x, g):
    return x * lax.rsqrt(jnp.mean(x * x, axis=-1, keepdims=True) + NORM_EPS) * g


def _ada_kernel(c_ref, w_ref, b_ref, o_ref):
    c = c_ref[...]
    ca = (c * jax.nn.sigmoid(c)).astype(BF16)
    o_ref[0] = jnp.dot(ca, w_ref[0].astype(BF16), preferred_element_type=F32) + b_ref[0]


def _ada_call(c, w_ada, b_ada):
    L, D, N = w_ada.shape
    B = c.shape[0]
    tn = ADA_TN
    vmem = 2 * (D * tn * 4) + D * tn * 2 + 4 * B * (D + 2 * tn) * 4 + (2 << 20)
    return pl.pallas_call(
        _ada_kernel,
        grid=(L, N // tn),
        in_specs=[
            pl.BlockSpec((B, D), lambda l, j: (0, 0)),
            pl.BlockSpec((1, D, tn), lambda l, j: (l, 0, j)),
            pl.BlockSpec((1, 1, tn), lambda l, j: (l, 0, j)),
        ],
        out_specs=pl.BlockSpec((1, B, tn), lambda l, j: (l, 0, j)),
        out_shape=jax.ShapeDtypeStruct((L, B, N), F32),
        compiler_params=_cparams(("arbitrary", "arbitrary"), vmem),
        name="ada_mod",
    )(c, w_ada, b_ada.reshape(L, 1, N))


_QK_BLOCKS = (0, 2)
_GQK_BLOCKS = (2, 3)
_ACT_BLOCKS = (3, 10)
_ACT_SIGMOID_CHUNKS = 8
_ACT_PLAIN_CHUNKS = 4


def _inproj_kernel(x_ref, mod_ref, g_ref, pos_ref, freq_ref, sgn_ref, w_ref, wlr_ref,
                   qk_ref, gqk_ref, act_ref, lr_ref, h_scr, cos_scr, sin_scr):
    j = pl.program_id(1)
    tm, tn = qk_ref.shape
    sub = INPROJ_SUB

    @pl.when(j == 0)
    def _prologue():
        y = _rms_norm(x_ref[...], g_ref[...])
        h = (y * (1.0 + mod_ref[0, 1:2, :]) + mod_ref[0, 0:1, :]).astype(BF16)
        h_scr[...] = h
        lr_ref[...] = jnp.dot(h, wlr_ref[...], preferred_element_type=F32)
        ang = pos_ref[...].astype(F32) * freq_ref[...]
        cos_scr[...] = jnp.cos(ang)
        sin_scr[...] = jnp.sin(ang) * sgn_ref[...]

    def sub_tiles(epilogue):
        for c in range(tn // sub):
            for r in range(tm // sub):
                rows = slice(r * sub, (r + 1) * sub)
                cols = slice(c * sub, (c + 1) * sub)
                acc = jnp.dot(h_scr[rows, :], w_ref[:, cols], preferred_element_type=F32)
                epilogue(rows, cols, c, acc)

    @pl.when(j < _QK_BLOCKS[1])
    def _qk():
        lane = lax.broadcasted_iota(jnp.int32, (sub, LANES), 1)
        half = ROT_DIM // 2
        qscale = jnp.where(j == 0, (DIFF_DQK ** -0.5) * LOG2E, 1.0).astype(F32)

        def epilogue(rows, cols, c, acc):
            cos = cos_scr[rows, :]
            sin = sin_scr[rows, :]
            for gi in range(sub // LANES):
                t = acc[:, gi * LANES:(gi + 1) * LANES]
                partner = jnp.where(lane < half, pltpu.roll(t, LANES - half, 1), pltpu.roll(t, half, 1))
                rot = jnp.where(lane < ROT_DIM, t * cos + partner * sin, t)
                lo = c * sub + gi * LANES
                qk_ref[rows, lo:lo + LANES] = (rot * qscale).astype(BF16)

        sub_tiles(epilogue)

    @pl.when(jnp.logical_and(j >= _GQK_BLOCKS[0], j < _GQK_BLOCKS[1]))
    def _gqk():
        def epilogue(rows, cols, c, acc):
            gqk_ref[rows, cols] = acc * (GLA_DK ** -0.5) if c == 0 else acc

        sub_tiles(epilogue)

    @pl.when(j >= _ACT_BLOCKS[0])
    def _act():
        def epilogue(rows, cols, c, acc):
            chunk = (j - _ACT_BLOCKS[0]) * (tn // sub) + c
            sg = jax.nn.sigmoid(acc)
            val = jnp.where(chunk < _ACT_SIGMOID_CHUNKS, sg,
                            jnp.where(chunk < _ACT_SIGMOID_CHUNKS + _ACT_PLAIN_CHUNKS, acc, acc * sg))
            act_ref[rows, cols] = val.astype(BF16)

        sub_tiles(epilogue)


def _inproj_call(x2, mod_l, norm_g, pos2, freq, sgn, w_main, w_lr, seq):
    M, D = x2.shape
    tm, tn = INPROJ_TM, INPROJ_TN
    nblk = w_main.shape[1] // tn
    assert nblk == _ACT_BLOCKS[1] and seq % tm == 0 and 2 * D == _ACT_SIGMOID_CHUNKS * INPROJ_SUB
    bpb = seq // tm

    def seg(rng):
        return lambda i, j: (i, jnp.clip(j - rng[0], 0, rng[1] - rng[0] - 1))

    def nseg(rng):
        return (rng[1] - rng[0]) * tn

    out_shape = (
        jax.ShapeDtypeStruct((M, nseg(_QK_BLOCKS)), BF16),
        jax.ShapeDtypeStruct((M, nseg(_GQK_BLOCKS)), F32),
        jax.ShapeDtypeStruct((M, nseg(_ACT_BLOCKS)), BF16),
        jax.ShapeDtypeStruct((M, LANES), F32),
    )
    out_specs = (
        pl.BlockSpec((tm, tn), seg(_QK_BLOCKS)),
        pl.BlockSpec((tm, tn), seg(_GQK_BLOCKS)),
        pl.BlockSpec((tm, tn), seg(_ACT_BLOCKS)),
        pl.BlockSpec((tm, LANES), lambda i, j: (i, 0)),
    )
    vmem = (2 * tm * D * 4 + tm * D * 2 + 2 * D * tn * 2 + 2 * D * LANES * 2
            + 2 * tm * tn * (2 + 4 + 2) + 2 * tm * LANES * 4 * 2 + 2 * tm * LANES * 4
            + 6 * INPROJ_SUB * INPROJ_SUB * 4 + (4 << 20))
    return pl.pallas_call(
        _inproj_kernel,
        grid=(M // tm, nblk),
        in_specs=[
            pl.BlockSpec((tm, D), lambda i, j: (i, 0)),
            pl.BlockSpec((1, 6, D), lambda i, j: (i // bpb, 0, 0)),
            pl.BlockSpec((1, D), lambda i, j: (0, 0)),
            pl.BlockSpec((tm, 1), lambda i, j: (i, 0)),
            pl.BlockSpec((1, LANES), lambda i, j: (0, 0)),
            pl.BlockSpec((1, LANES), lambda i, j: (0, 0)),
            pl.BlockSpec((D, tn), lambda i, j: (0, j)),
            pl.BlockSpec((D, LANES), lambda i, j: (0, 0)),
        ],
        out_specs=out_specs,
        out_shape=out_shape,
        scratch_shapes=[
            pltpu.VMEM((tm, D), BF16),
            pltpu.VMEM((tm, LANES), F32),
            pltpu.VMEM((tm, LANES), F32),
        ],
        compiler_params=_cparams(("arbitrary", "arbitrary"), vmem),
        name="in_proj",
    )(x2, mod_l, norm_g, pos2, freq, sgn, w_main, w_lr)


def _attn_kernel(lam_init, q_ref, k_ref, v_ref, lqk_ref, g_ref, o_ref):
    S = k_ref.shape[0]
    kc = ATTN_KC
    lqk = lqk_ref[...]
    lam = (jnp.exp(jnp.sum(lqk[0:1] * lqk[1:2], axis=-1, keepdims=True))
           - jnp.exp(jnp.sum(lqk[2:3] * lqk[3:4], axis=-1, keepdims=True)) + lam_init)

    m = [None, None]
    l = [None, None]
    acc = [None, None]
    for c in range(S // kc):
        keys = slice(c * kc, (c + 1) * kc)
        for mi in range(2):
            sl = slice(mi * DIFF_DQK, (mi + 1) * DIFF_DQK)
            s = _nt_dot(q_ref[:, sl], k_ref[keys, sl])
            smax = jnp.max(s, axis=-1, keepdims=True)
            if c == 0:
                m[mi] = smax
                p = jnp.exp2(s - smax)
                l[mi] = jnp.sum(p, axis=-1, keepdims=True)
                acc[mi] = jnp.dot(p.astype(BF16), v_ref[keys, :], preferred_element_type=F32)
            else:
                m_new = jnp.maximum(m[mi], smax)
                alpha = jnp.exp2(m[mi] - m_new)
                p = jnp.exp2(s - m_new)
                l[mi] = alpha * l[mi] + jnp.sum(p, axis=-1, keepdims=True)
                acc[mi] = alpha * acc[mi] + jnp.dot(p.astype(BF16), v_ref[keys, :], preferred_element_type=F32)
                m[mi] = m_new

    o = acc[0] / l[0] - lam * (acc[1] / l[1])
    o_ref[...] = (_rms_norm(o, g_ref[...]) * (1.0 - lam_init)).astype(BF16)


def _attn_call(qk, act, lqk, subln_g, lam_init, batch, seq, v_block0):
    M = qk.shape[0]
    tq = ATTN_TQ
    nq = seq // tq
    H = DIFF_HEADS
    w = 2 * DIFF_DQK
    vmem = (2 * tq * w * 2 * 2 + 2 * 2 * seq * w * 2 + 6 * tq * ATTN_KC * 4 + 6 * tq * DIFF_DV * 4 + (4 << 20))
    return pl.pallas_call(
        functools.partial(_attn_kernel, lam_init),
        grid=(batch, H, nq),
        in_specs=[
            pl.BlockSpec((tq, w), lambda b, h, i: (b * nq + i, h)),
            pl.BlockSpec((seq, w), lambda b, h, i: (b, H + h)),
            pl.BlockSpec((seq, DIFF_DV), lambda b, h, i: (b, v_block0 + h)),
            pl.BlockSpec((4, DIFF_DQK), lambda b, h, i: (0, 0)),
            pl.BlockSpec((1, DIFF_DV), lambda b, h, i: (0, 0)),
        ],
        out_specs=pl.BlockSpec((tq, DIFF_DV), lambda b, h, i: (b * nq + i, h)),
        out_shape=jax.ShapeDtypeStruct((M, H * DIFF_DV), BF16),
        compiler_params=_cparams(("arbitrary", "arbitrary", "arbitrary"), vmem),
        name="diff_attn",
    )(qk, qk, act, lqk, subln_g)


def _gla_kernel(q_ref, k_ref, v_ref, grs_ref, lr_ref, w2f_ref, w2b_ref, bf_ref, bb_ref, g_ref, o_ref,
                qt_scr, kt_scr, kh_scr, d_scr, acc_scr):
    S = q_ref.shape[0]
    C = GLA_CHUNK
    R = GLA_ROWS
    nc = S // C
    cpr = R // C
    shift = int(math.log2(C))

    rr = lax.broadcasted_iota(jnp.int32, (R, R), 0)
    cc = lax.broadcasted_iota(jnp.int32, (R, R), 1)
    same = (rr >> shift) == (cc >> shift)
    ones_bd = jnp.where(same, 1.0, 0.0).astype(BF16)
    cum = (jnp.where(jnp.logical_and(same, cc <= rr), 1.0, 0.0).astype(BF16),
           jnp.where(jnp.logical_and(same, cc >= rr), 1.0, 0.0).astype(BF16))
    r1 = lax.broadcasted_iota(jnp.int32, (C, C), 0)
    c1 = lax.broadcasted_iota(jnp.int32, (C, C), 1)
    masks = (c1 <= r1, c1 >= r1)
    w2 = (w2f_ref[...], w2b_ref[...])
    bias = (bf_ref[...], bb_ref[...])

    def split_dot(mat, a):
        hi = a.astype(BF16)
        lo = (a - hi.astype(F32)).astype(BF16)
        return jnp.dot(mat, hi, preferred_element_type=F32) + jnp.dot(mat, lo, preferred_element_type=F32)

    def pre(i, carry):
        rows = pl.ds(pl.multiple_of(i * R, R), R)
        lr = lr_ref[rows, :].astype(BF16)
        q = q_ref[rows, :]
        k = k_ref[rows, :]
        for di in range(2):
            xg = jnp.dot(lr, w2[di], preferred_element_type=F32) + bias[di]
            log_a = (jnp.minimum(xg, 0.0) - jnp.log1p(jnp.exp(-jnp.abs(xg)))) * (1.0 / GLA_TAU)
            b = split_dot(cum[di], log_a)
            tot = split_dot(ones_bd, log_a)
            qt_scr[di, rows, :] = (q * jnp.exp(b)).astype(BF16)
            kt_scr[di, rows, :] = (k * jnp.exp(-b)).astype(BF16)
            kh_scr[di, rows, :] = (k * jnp.exp(tot - b)).astype(BF16)
            for ci in range(cpr):
                drow = pl.ds(pl.multiple_of((i * cpr + ci) * SUBLANES, SUBLANES), SUBLANES)
                d_scr[di, drow, :] = jnp.exp(tot[ci * C:ci * C + SUBLANES, :])
        return carry

    lax.fori_loop(0, S // R, pre, 0, unroll=2)

    def chunk_step(di, t, st, first_touch):
        rows = pl.ds(pl.multiple_of(t * C, C), C)
        qt = qt_scr[di, rows, :]
        v = v_ref[rows, :]
        sc = jnp.where(masks[di], _nt_dot(qt, kt_scr[di, rows, :]), 0.0).astype(BF16)
        o = _nt_dot(qt, st.astype(BF16)) + jnp.dot(sc, v, preferred_element_type=F32)
        if first_touch:
            acc_scr[rows, :] = o
        else:
            acc_scr[rows, :] += o
        d = d_scr[di, pl.ds(pl.multiple_of(t * SUBLANES, SUBLANES), 1), :]
        return st * d + _tn_dot(v, kh_scr[di, rows, :])

    def sweep(first_touch):
        def body(i, sts):
            return (chunk_step(0, i, sts[0], first_touch), chunk_step(1, nc - 1 - i, sts[1], first_touch))
        return body

    zero = jnp.zeros((GLA_DV, GLA_DK), F32)
    sts = lax.fori_loop(0, nc // 2, sweep(True), (zero, zero), unroll=GLA_UNROLL)
    lax.fori_loop(nc // 2, nc, sweep(False), sts, unroll=GLA_UNROLL)

    def fin(i, carry):
        rows = pl.ds(pl.multiple_of(i * R, R), R)
        y = _rms_norm(acc_scr[rows, :], g_ref[...])
        o_ref[rows, :] = (y * grs_ref[rows, :].astype(F32)).astype(BF16)
        return carry

    lax.fori_loop(0, S // R, fin, 0, unroll=2)


def _gla_call(gqk, act, lr, w2f_pad, w2b_pad, bf, bb, gla_g, batch, seq, v_block0, g_block0):
    M = gqk.shape[0]
    H = GLA_HEADS
    nc = seq // GLA_CHUNK
    vmem = (2 * (2 * seq * GLA_DK * 4 + 2 * seq * GLA_DV * 2 + seq * LANES * 4 + seq * GLA_DV * 2)
            + 2 * 3 * seq * GLA_DK * 2 + 2 * nc * SUBLANES * GLA_DK * 4 + seq * GLA_DV * 4 + (8 << 20))
    return pl.pallas_call(
        _gla_kernel,
        grid=(batch, H),
        in_specs=[
            pl.BlockSpec((seq, GLA_DK), lambda b, h: (b, h)),
            pl.BlockSpec((seq, GLA_DK), lambda b, h: (b, H + h)),
            pl.BlockSpec((seq, GLA_DV), lambda b, h: (b, v_block0 + h)),
            pl.BlockSpec((seq, GLA_DV), lambda b, h: (b, g_block0 + h)),
            pl.BlockSpec((seq, LANES), lambda b, h: (b, 0)),
            pl.BlockSpec((LANES, GLA_DK), lambda b, h: (0, h)),
            pl.BlockSpec((LANES, GLA_DK), lambda b, h: (0, h)),
            pl.BlockSpec((1, GLA_DK), lambda b, h: (0, h)),
            pl.BlockSpec((1, GLA_DK), lambda b, h: (0, h)),
            pl.BlockSpec((1, GLA_DV), lambda b, h: (0, 0)),
        ],
        out_specs=pl.BlockSpec((seq, GLA_DV), lambda b, h: (b, h)),
        out_shape=jax.ShapeDtypeStruct((M, H * GLA_DV), BF16),
        scratch_shapes=[
            pltpu.VMEM((2, seq, GLA_DK), BF16),
            pltpu.VMEM((2, seq, GLA_DK), BF16),
            pltpu.VMEM((2, seq, GLA_DK), BF16),
            pltpu.VMEM((2, nc * SUBLANES, GLA_DK), F32),
            pltpu.VMEM((seq, GLA_DV), F32),
        ],
        compiler_params=_cparams(("arbitrary", "arbitrary"), vmem),
        name="gla",
    )(gqk, gqk, act, act, lr, w2f_pad, w2b_pad, bf, bb, gla_g)


def _post_kernel(od_ref, og_ref, ga_ref, gb_ref, x_ref, mod_ref, wbd_ref, wbg_ref, wo_ref, xo_ref):
    yd = jnp.dot(od_ref[...], wbd_ref[...], preferred_element_type=F32)
    yg = jnp.dot(og_ref[...], wbg_ref[...], preferred_element_type=F32)
    merged = (ga_ref[...].astype(F32) * yd + gb_ref[...].astype(F32) * yg).astype(BF16)
    out = jnp.dot(merged, wo_ref[...], preferred_element_type=F32)
    xo_ref[...] = x_ref[...] + mod_ref[0, 2:3, :] * out


def _post_call(od, og, act, x2, mod_l, wbd, wbg, wo, seq):
    M, D = x2.shape
    tm = POST_TM
    bpb = seq // tm
    kd, kg = od.shape[1], og.shape[1]
    const = dict(pipeline_mode=pl.Buffered(1))
    vmem = ((kd + kg + D) * D * 2 + 2 * tm * (kd + kg) * 2 + 2 * 2 * tm * D * 2 + 2 * 2 * tm * D * 4
            + 4 * tm * D * 4 + (4 << 20))
    return pl.pallas_call(
        _post_kernel,
        grid=(M // tm,),
        in_specs=[
            pl.BlockSpec((tm, kd), lambda i: (i, 0)),
            pl.BlockSpec((tm, kg), lambda i: (i, 0)),
            pl.BlockSpec((tm, D), lambda i: (i, 0)),
            pl.BlockSpec((tm, D), lambda i: (i, 1)),
            pl.BlockSpec((tm, D), lambda i: (i, 0)),
            pl.BlockSpec((1, 6, D), lambda i: (i // bpb, 0, 0)),
            pl.BlockSpec((kd, D), lambda i: (0, 0), **const),
            pl.BlockSpec((kg, D), lambda i: (0, 0), **const),
            pl.BlockSpec((D, D), lambda i: (0, 0), **const),
        ],
        out_specs=pl.BlockSpec((tm, D), lambda i: (i, 0)),
        out_shape=jax.ShapeDtypeStruct((M, D), F32),
        compiler_params=_cparams(("arbitrary",), vmem),
        name="mix_out",
    )(od, og, act, act, x2, mod_l, wbd, wbg, wo)


def _ffn_kernel(final_norm, bpb, x_ref, xp_ref, xn_ref, mod_ref, g_ref, wg_ref, wu_ref, cw_ref, cb_ref, wd_ref, fg_ref,
                o_ref, h_scr, g_scr, acc_scr):
    i = pl.program_id(0)
    j = pl.program_id(1)
    nj = pl.num_programs(1)
    tm, D = x_ref.shape
    hl = FFN_HALO

    def modulated(xv):
        return _rms_norm(xv, g_ref[...]) * (1.0 + mod_ref[0, 4:5, :]) + mod_ref[0, 3:4, :]

    @pl.when(j == 0)
    def _prologue():
        in_seq_prev = (i % bpb) != 0
        in_seq_next = (i % bpb) != (bpb - 1)
        h_scr[0:hl, :] = jnp.where(in_seq_prev, modulated(xp_ref[...]), 0.0).astype(BF16)
        h_scr[hl:hl + tm, :] = modulated(x_ref[...]).astype(BF16)
        h_scr[hl + tm:hl + tm + hl, :] = jnp.where(in_seq_next, modulated(xn_ref[...]), 0.0).astype(BF16)
        acc_scr[...] = jnp.zeros_like(acc_scr)

    g_scr[...] = jnp.dot(h_scr[...], wg_ref[...], preferred_element_type=F32)
    u = jnp.dot(h_scr[hl:hl + tm, :], wu_ref[...], preferred_element_type=F32)
    cw = cw_ref[...]
    gc = (cw[0:1] * g_scr[pl.ds(hl - 1, tm), :] + cw[1:2] * g_scr[pl.ds(hl, tm), :]
          + cw[2:3] * g_scr[pl.ds(hl + 1, tm), :] + cb_ref[...])
    act = (gc * jax.nn.sigmoid(gc) * u).astype(BF16)
    for c in range(D // FFN_NC):
        cols = slice(c * FFN_NC, (c + 1) * FFN_NC)
        acc_scr[:, cols] += jnp.dot(act, wd_ref[:, cols], preferred_element_type=F32)

    @pl.when(j == nj - 1)
    def _epilogue():
        xo = x_ref[...] + mod_ref[0, 5:6, :] * acc_scr[...]
        if final_norm:
            xo = _rms_norm(xo, fg_ref[...])
        o_ref[...] = xo


def _ffn_call(x2, mod_l, norm_g, wg, wu, conv_w, conv_b, wd, final_g, final_norm, seq):
    M, D = x2.shape
    Fdim = wg.shape[1]
    tm, tn, hl = FFN_TM, FFN_TN, FFN_HALO
    assert seq % tm == 0 and Fdim % tn == 0 and tm % hl == 0 and D % FFN_NC == 0
    bpb = seq // tm
    hpb = tm // hl
    nh = M // hl
    vmem = (2 * tm * D * 4 * 2 + 2 * 2 * hl * D * 4 + 2 * 3 * D * tn * 2 + (tm + 2 * hl) * D * 2
            + (tm + 2 * hl) * tn * 4 + tm * D * 4 + 6 * tm * tn * 4 + tm * D * 4 + (4 << 20))
    return pl.pallas_call(
        functools.partial(_ffn_kernel, final_norm, bpb),
        grid=(M // tm, Fdim // tn),
        in_specs=[
            pl.BlockSpec((tm, D), lambda i, j: (i, 0)),
            pl.BlockSpec((hl, D), lambda i, j: (jnp.maximum(i * hpb - 1, 0), 0)),
            pl.BlockSpec((hl, D), lambda i, j: (jnp.minimum((i + 1) * hpb, nh - 1), 0)),
            pl.BlockSpec((1, 6, D), lambda i, j: (i // bpb, 0, 0)),
            pl.BlockSpec((1, D), lambda i, j: (0, 0)),
            pl.BlockSpec((D, tn), lambda i, j: (0, j)),
            pl.BlockSpec((D, tn), lambda i, j: (0, j)),
            pl.BlockSpec((CONV_WIDTH, tn), lambda i, j: (0, j)),
            pl.BlockSpec((1, tn), lambda i, j: (0, j)),
            pl.BlockSpec((tn, D), lambda i, j: (j, 0)),
            pl.BlockSpec((1, D), lambda i, j: (0, 0)),
        ],
        out_specs=pl.BlockSpec((tm, D), lambda i, j: (i, 0)),
        out_shape=jax.ShapeDtypeStruct((M, D), F32),
        scratch_shapes=[
            pltpu.VMEM((tm + 2 * hl, D), BF16),
            pltpu.VMEM((tm + 2 * hl, tn), F32),
            pltpu.VMEM((tm, D), F32),
        ],
        compiler_params=_cparams(("arbitrary", "arbitrary"), vmem),
        name="conv_ffn",
    )(x2, x2, x2, mod_l, norm_g, wg, wu, conv_w, conv_b, wd, final_g)


def kernel(x, c, positions, w_ada, b_ada, norm_mix_g, w_in, lambda_q1, lambda_k1, lambda_q2, lambda_k2,
           diff_subln_g, gla_w2_fwd, gla_b_fwd, gla_w2_bwd, gla_b_bwd, gla_norm_g, w_branch_diff,
           w_branch_gla, w_out, norm_ffn_g, w_gate, w_up, conv_w, conv_b, w_down, final_norm_g):
    B, S, D = x.shape
    L = w_ada.shape[0]
    M = B * S
    x2 = x.reshape(M, D)
    pos2 = positions.reshape(M, 1).astype(jnp.int32)

    half = ROT_DIM // 2
    inv_freq = ROPE_THETA ** (-jnp.arange(0, ROT_DIM, 2, dtype=F32) / ROT_DIM)
    freq = jnp.zeros((1, LANES), F32).at[0, :half].set(inv_freq).at[0, half:ROT_DIM].set(inv_freq)
    sgn = jnp.zeros((1, LANES), F32).at[0, :half].set(-1.0).at[0, half:ROT_DIM].set(1.0)

    mod = _ada_call(c, w_ada, b_ada).reshape(L, B, 6, D)

    sizes = (DIFF_HEADS * 2 * DIFF_DQK, DIFF_HEADS * 2 * DIFF_DQK, DIFF_HEADS * DIFF_DV,
             GLA_HEADS * GLA_DK, GLA_HEADS * GLA_DK, GLA_HEADS * GLA_DV, GLA_HEADS * GLA_DV,
             GLA_GATE_RANK, GLA_GATE_RANK, D, D)
    offs = [0]
    for s_ in sizes:
        offs.append(offs[-1] + s_)
    dv_block0 = 2 * D // DIFF_DV
    gv_block0 = dv_block0 + DIFF_HEADS
    gr_block0 = gv_block0 + GLA_HEADS

    for l in range(L):
        wl = w_in[l]
        cols = [wl[:, offs[k]:offs[k + 1]] for k in range(len(sizes))]
        w_main = jnp.concatenate([cols[0], cols[1], cols[3], cols[4], cols[9], cols[10], cols[2], cols[5], cols[6]],
                                 axis=1).astype(BF16)
        w_lr = jnp.zeros((D, LANES), F32).at[:, :GLA_GATE_RANK].set(cols[7])
        w_lr = w_lr.at[:, GLA_GATE_RANK:2 * GLA_GATE_RANK].set(cols[8]).astype(BF16)
        nk = GLA_HEADS * GLA_DK
        w2f_pad = jnp.zeros((LANES, nk), F32).at[:GLA_GATE_RANK].set(gla_w2_fwd[l]).astype(BF16)
        w2b_pad = jnp.zeros((LANES, nk), F32).at[GLA_GATE_RANK:2 * GLA_GATE_RANK].set(gla_w2_bwd[l]).astype(BF16)
        lam_init = 0.8 - 0.6 * math.exp(-0.3 * l)
        lqk = jnp.stack([lambda_q1[l], lambda_k1[l], lambda_q2[l], lambda_k2[l]])

        qk, gqk, act, lr = _inproj_call(
            x2, mod[l], norm_mix_g[l].reshape(1, D), pos2, freq, sgn, w_main, w_lr, S)
        od = _attn_call(qk, act, lqk, diff_subln_g[l].reshape(1, DIFF_DV), lam_init, B, S, dv_block0)
        og = _gla_call(gqk, act, lr, w2f_pad, w2b_pad, gla_b_fwd[l].reshape(1, nk),
                       gla_b_bwd[l].reshape(1, nk), gla_norm_g[l].reshape(1, GLA_DV), B, S, gv_block0, gr_block0)
        x2 = _post_call(od, og, act, x2, mod[l], w_branch_diff[l].astype(BF16),
                        w_branch_gla[l].astype(BF16), w_out[l].astype(BF16), S)
        x2 = _ffn_call(x2, mod[l], norm_ffn_g[l].reshape(1, D), w_gate[l].astype(BF16), w_up[l].astype(BF16),
                       conv_w[l], conv_b[l].reshape(1, -1), w_down[l].astype(BF16),
                       final_norm_g.reshape(1, D), l == L - 1, S)
    return x2.reshape(B, S, D)
```

```python
import functools
import math

import jax
import jax.numpy as jnp
from jax import lax
from jax.experimental import pallas as pl
from jax.experimental.pallas import tpu as pltpu

F32 = jnp.float32
BF16 = jnp.bfloat16

DIFF_HEADS = 4
DIFF_DQK = 128
DIFF_DV = 2 * DIFF_DQK
ROT_DIM = DIFF_DQK // 4
ROPE_THETA = 500000.0
GLA_HEADS = 4
GLA_DK = 128
GLA_DV = 256
GLA_GATE_RANK = 16
GLA_TAU = 16.0
GLA_CHUNK = 64
CONV_WIDTH = 3
NORM_EPS = 1e-6
LOG2E = 1.4426950408889634

LANES = 128
SUBLANES = 8
BF16_SUBLANES = 16
VMEM_BUDGET_BYTES = 58 * 1024 * 1024

ADA_TN = 1536
CAST_BLOCK_BYTES = 4 * 1024 * 1024
INPROJ_TM = 1024
INPROJ_TN = 1024
INPROJ_SUB = 512
ATTN_TQ = 512
ATTN_KC = 1024
GLA_ROWS = 256
GLA_PRE_BLOCKS = 2
GLA_GROUP = 4
POST_TM = 256
FFN_TM = 512
FFN_TN = 512
FFN_NC = 512
FFN_SLOTS = 3
FFN_HALO = BF16_SUBLANES


def _cparams(sem, vmem_bytes):
    return pltpu.CompilerParams(dimension_semantics=sem, vmem_limit_bytes=min(int(vmem_bytes), VMEM_BUDGET_BYTES))


def _nt_dot(a, b):
    return lax.dot_general(a, b, (((1,), (1,)), ((), ())), preferred_element_type=F32)


def _tn_dot(a, b):
    return lax.dot_general(a, b, (((0,), (0,)), ((), ())), preferred_element_type=F32)


def _rms_norm(x, g):
    return x * lax.rsqrt(jnp.mean(x * x, axis=-1, keepdims=True) + NORM_EPS) * g


def _ada_kernel(c_ref, w_ref, b_ref, o_ref):
    c = c_ref[...]
    ca = (c * jax.nn.sigmoid(c)).astype(BF16)
    o_ref[0] = jnp.dot(ca, w_ref[0].astype(BF16), preferred_element_type=F32) + b_ref[0]


def _ada_call(c, w_ada, b_ada):
    L, D, N = w_ada.shape
    B = c.shape[0]
    tn = ADA_TN
    vmem = 2 * (D * tn * 4) + D * tn * 2 + 4 * B * (D + 2 * tn) * 4 + (2 << 20)
    return pl.pallas_call(
        _ada_kernel,
        grid=(L, N // tn),
        in_specs=[
            pl.BlockSpec((B, D), lambda l, j: (0, 0)),
            pl.BlockSpec((1, D, tn), lambda l, j: (l, 0, j)),
            pl.BlockSpec((1, 1, tn), lambda l, j: (l, 0, j)),
        ],
        out_specs=pl.BlockSpec((1, B, tn), lambda l, j: (l, 0, j)),
        out_shape=jax.ShapeDtypeStruct((L, B, N), F32),
        compiler_params=_cparams(("arbitrary", "arbitrary"), vmem),
        name="ada_mod",
    )(c, w_ada, b_ada.reshape(L, 1, N))


def _cast_kernel(w_ref, o_ref):
    o_ref[...] = w_ref[0].astype(BF16)


def _cast_call(w3, l):
    _, R, C = w3.shape
    limit = max(BF16_SUBLANES, CAST_BLOCK_BYTES // (C * 4))
    tr = max(t for t in range(BF16_SUBLANES, min(R, limit) + 1, BF16_SUBLANES) if R % t == 0)
    return pl.pallas_call(
        _cast_kernel,
        grid=(R // tr,),
        in_specs=[pl.BlockSpec((1, tr, C), lambda i: (l, i, 0))],
        out_specs=pl.BlockSpec((tr, C), lambda i: (i, 0)),
        out_shape=jax.ShapeDtypeStruct((R, C), BF16),
        compiler_params=_cparams(("arbitrary",), 2 * tr * C * (4 + 2) + (2 << 20)),
        name="cast_bf16",
    )(w3)


_QK_BLOCKS = (0, 2)
_GQK_BLOCKS = (2, 3)
_ACT_BLOCKS = (3, 10)
_ACT_SIGMOID_CHUNKS = 8
_ACT_PLAIN_CHUNKS = 4


def _inproj_kernel(x_ref, mod_ref, g_ref, pos_ref, freq_ref, sgn_ref, w_ref, wlr_ref,
                   qk_ref, gqk_ref, act_ref, lr_ref, h_scr, cos_scr, sin_scr):
    j = pl.program_id(1)
    tm, tn = qk_ref.shape
    sub = INPROJ_SUB

    @pl.when(j == 0)
    def _prologue():
        y = _rms_norm(x_ref[...], g_ref[...])
        h = (y * (1.0 + mod_ref[0, 1:2, :]) + mod_ref[0, 0:1, :]).astype(BF16)
        h_scr[...] = h
        lr_ref[...] = jnp.dot(h, wlr_ref[...], preferred_element_type=F32)
        ang = pos_ref[...].astype(F32) * freq_ref[...]
        cos_scr[...] = jnp.cos(ang)
        sin_scr[...] = jnp.sin(ang) * sgn_ref[...]

    def sub_tiles(epilogue):
        for c in range(tn // sub):
            for r in range(tm // sub):
                rows = slice(r * sub, (r + 1) * sub)
                cols = slice(c * sub, (c + 1) * sub)
                acc = jnp.dot(h_scr[rows, :], w_ref[:, cols], preferred_element_type=F32)
                epilogue(rows, cols, c, acc)

    @pl.when(j < _QK_BLOCKS[1])
    def _qk():
        lane = lax.broadcasted_iota(jnp.int32, (sub, LANES), 1)
        half = ROT_DIM // 2
        qscale = jnp.where(j == 0, (DIFF_DQK ** -0.5) * LOG2E, 1.0).astype(F32)

        def epilogue(rows, cols, c, acc):
            cos = cos_scr[rows, :]
            sin = sin_scr[rows, :]
            for gi in range(sub // LANES):
                t = acc[:, gi * LANES:(gi + 1) * LANES]
                partner = jnp.where(lane < half, pltpu.roll(t, LANES - half, 1), pltpu.roll(t, half, 1))
                rot = jnp.where(lane < ROT_DIM, t * cos + partner * sin, t)
                lo = c * sub + gi * LANES
                qk_ref[rows, lo:lo + LANES] = (rot * qscale).astype(BF16)

        sub_tiles(epilogue)

    @pl.when(jnp.logical_and(j >= _GQK_BLOCKS[0], j < _GQK_BLOCKS[1]))
    def _gqk():
        def epilogue(rows, cols, c, acc):
            gqk_ref[rows, cols] = acc * (GLA_DK ** -0.5) if c == 0 else acc

        sub_tiles(epilogue)

    @pl.when(j >= _ACT_BLOCKS[0])
    def _act():
        def epilogue(rows, cols, c, acc):
            chunk = (j - _ACT_BLOCKS[0]) * (tn // sub) + c
            sg = jax.nn.sigmoid(acc)
            val = jnp.where(chunk < _ACT_SIGMOID_CHUNKS, sg,
                            jnp.where(chunk < _ACT_SIGMOID_CHUNKS + _ACT_PLAIN_CHUNKS, acc, acc * sg))
            act_ref[rows, cols] = val.astype(BF16)

        sub_tiles(epilogue)


def _inproj_call(x2, mod_l, norm_g, pos2, freq, sgn, w_main, w_lr, seq):
    M, D = x2.shape
    tm, tn = INPROJ_TM, INPROJ_TN
    nblk = w_main.shape[1] // tn
    assert nblk == _ACT_BLOCKS[1] and seq % tm == 0 and 2 * D == _ACT_SIGMOID_CHUNKS * INPROJ_SUB
    bpb = seq // tm

    def seg(rng):
        return lambda i, j: (i, jnp.clip(j - rng[0], 0, rng[1] - rng[0] - 1))

    def nseg(rng):
        return (rng[1] - rng[0]) * tn

    out_shape = (
        jax.ShapeDtypeStruct((M, nseg(_QK_BLOCKS)), BF16),
        jax.ShapeDtypeStruct((M, nseg(_GQK_BLOCKS)), F32),
        jax.ShapeDtypeStruct((M, nseg(_ACT_BLOCKS)), BF16),
        jax.ShapeDtypeStruct((M, LANES), F32),
    )
    out_specs = (
        pl.BlockSpec((tm, tn), seg(_QK_BLOCKS)),
        pl.BlockSpec((tm, tn), seg(_GQK_BLOCKS)),
        pl.BlockSpec((tm, tn), seg(_ACT_BLOCKS)),
        pl.BlockSpec((tm, LANES), lambda i, j: (i, 0)),
    )
    vmem = (2 * tm * D * 4 + tm * D * 2 + 2 * D * tn * 2 + 2 * D * LANES * 2
            + 2 * tm * tn * (2 + 4 + 2) + 2 * tm * LANES * 4 * 2 + 2 * tm * LANES * 4
            + 6 * INPROJ_SUB * INPROJ_SUB * 4 + (4 << 20))
    return pl.pallas_call(
        _inproj_kernel,
        grid=(M // tm, nblk),
        in_specs=[
            pl.BlockSpec((tm, D), lambda i, j: (i, 0)),
            pl.BlockSpec((1, 6, D), lambda i, j: (i // bpb, 0, 0)),
            pl.BlockSpec((1, D), lambda i, j: (0, 0)),
            pl.BlockSpec((tm, 1), lambda i, j: (i, 0)),
            pl.BlockSpec((1, LANES), lambda i, j: (0, 0)),
            pl.BlockSpec((1, LANES), lambda i, j: (0, 0)),
            pl.BlockSpec((D, tn), lambda i, j: (0, j)),
            pl.BlockSpec((D, LANES), lambda i, j: (0, 0)),
        ],
        out_specs=out_specs,
        out_shape=out_shape,
        scratch_shapes=[
            pltpu.VMEM((tm, D), BF16),
            pltpu.VMEM((tm, LANES), F32),
            pltpu.VMEM((tm, LANES), F32),
        ],
        compiler_params=_cparams(("arbitrary", "arbitrary"), vmem),
        name="in_proj",
    )(x2, mod_l, norm_g, pos2, freq, sgn, w_main, w_lr)


def _attn_kernel(lam_init, q_ref, k_ref, v_ref, lqk_ref, g_ref, o_ref):
    S = k_ref.shape[0]
    kc = ATTN_KC
    lqk = lqk_ref[...]
    lam = (jnp.exp(jnp.sum(lqk[0:1] * lqk[1:2], axis=-1, keepdims=True))
           - jnp.exp(jnp.sum(lqk[2:3] * lqk[3:4], axis=-1, keepdims=True)) + lam_init)

    items = [(c, mi) for c in range(S // kc) for mi in range(2)]

    def scores(item):
        c, mi = item
        sl = slice(mi * DIFF_DQK, (mi + 1) * DIFF_DQK)
        return _nt_dot(q_ref[:, sl], k_ref[c * kc:(c + 1) * kc, sl])

    m = [None, None]
    l = [None, None]
    acc = [None, None]
    s_next = scores(items[0])
    for n, (c, mi) in enumerate(items):
        s = s_next
        if n + 1 < len(items):
            s_next = scores(items[n + 1])
        v = v_ref[c * kc:(c + 1) * kc, :]
        smax = jnp.max(s, axis=-1, keepdims=True)
        if c == 0:
            m[mi] = smax
            p = jnp.exp2(s - smax)
            l[mi] = jnp.sum(p, axis=-1, keepdims=True)
            acc[mi] = jnp.dot(p.astype(BF16), v, preferred_element_type=F32)
        else:
            m_new = jnp.maximum(m[mi], smax)
            alpha = jnp.exp2(m[mi] - m_new)
            p = jnp.exp2(s - m_new)
            l[mi] = alpha * l[mi] + jnp.sum(p, axis=-1, keepdims=True)
            acc[mi] = alpha * acc[mi] + jnp.dot(p.astype(BF16), v, preferred_element_type=F32)
            m[mi] = m_new

    o = acc[0] / l[0] - lam * (acc[1] / l[1])
    o_ref[...] = (_rms_norm(o, g_ref[...]) * (1.0 - lam_init)).astype(BF16)


def _attn_call(qk, act, lqk, subln_g, lam_init, batch, seq, v_block0):
    M = qk.shape[0]
    tq = ATTN_TQ
    nq = seq // tq
    H = DIFF_HEADS
    w = 2 * DIFF_DQK
    vmem = (2 * tq * w * 2 * 2 + 2 * 2 * seq * w * 2 + 6 * tq * ATTN_KC * 4 + 6 * tq * DIFF_DV * 4 + (4 << 20))
    return pl.pallas_call(
        functools.partial(_attn_kernel, lam_init),
        grid=(batch, H, nq),
        in_specs=[
            pl.BlockSpec((tq, w), lambda b, h, i: (b * nq + i, h)),
            pl.BlockSpec((seq, w), lambda b, h, i: (b, H + h)),
            pl.BlockSpec((seq, DIFF_DV), lambda b, h, i: (b, v_block0 + h)),
            pl.BlockSpec((4, DIFF_DQK), lambda b, h, i: (0, 0)),
            pl.BlockSpec((1, DIFF_DV), lambda b, h, i: (0, 0)),
        ],
        out_specs=pl.BlockSpec((tq, DIFF_DV), lambda b, h, i: (b * nq + i, h)),
        out_shape=jax.ShapeDtypeStruct((M, H * DIFF_DV), BF16),
        compiler_params=_cparams(("arbitrary", "arbitrary", "arbitrary"), vmem),
        name="diff_attn",
    )(qk, qk, act, lqk, subln_g)


def _gla_kernel(q_ref, k_ref, v_ref, grs_ref, lr_ref, w2f_ref, w2b_ref, bf_ref, bb_ref, g_ref, o_ref,
                qt_scr, kt_scr, kh_scr, d_scr, acc_scr):
    S = q_ref.shape[0]
    C = GLA_CHUNK
    R = GLA_ROWS
    nc = S // C
    cpr = R // C
    shift = int(math.log2(C))

    rr = lax.broadcasted_iota(jnp.int32, (R, R), 0)
    cc = lax.broadcasted_iota(jnp.int32, (R, R), 1)
    same = (rr >> shift) == (cc >> shift)
    ones_bd = jnp.where(same, 1.0, 0.0)
    stack = tuple(
        jnp.concatenate([jnp.where(jnp.logical_and(same, keep), 1.0, 0.0), ones_bd], axis=0).astype(BF16)
        for keep in (cc <= rr, cc >= rr))
    r1 = lax.broadcasted_iota(jnp.int32, (C, C), 0)
    c1 = lax.broadcasted_iota(jnp.int32, (C, C), 1)
    masks = (c1 <= r1, c1 >= r1)
    w2 = (w2f_ref[...], w2b_ref[...])
    bias = (bf_ref[...], bb_ref[...])
    dk = GLA_DK

    def pre(i, carry):
        items = [(bi, di) for bi in range(GLA_PRE_BLOCKS) for di in range(2)]
        rows = [pl.ds(pl.multiple_of((i * GLA_PRE_BLOCKS + bi) * R, R), R) for bi in range(GLA_PRE_BLOCKS)]
        lr = [lr_ref[r, :].astype(BF16) for r in rows]
        xg = [jnp.dot(lr[bi], w2[di], preferred_element_type=F32) + bias[di] for bi, di in items]
        hilo = []
        for x in xg:
            log_a = (jnp.minimum(x, 0.0) - jnp.log(1.0 + jnp.exp(-jnp.abs(x)))) * (1.0 / GLA_TAU)
            hi = log_a.astype(BF16)
            lo = (log_a - hi.astype(F32)).astype(BF16)
            hilo.append(jnp.concatenate([hi, lo], axis=1))
        sums = [jnp.dot(stack[di], hilo[n], preferred_element_type=F32) for n, (bi, di) in enumerate(items)]
        for n, (bi, di) in enumerate(items):
            sm = sums[n][:, :dk] + sums[n][:, dk:]
            b = sm[:R]
            tot = sm[R:]
            q = q_ref[rows[bi], :]
            k = k_ref[rows[bi], :]
            qt_scr[di, rows[bi], :] = (q * jnp.exp(b)).astype(BF16)
            kt_scr[di, rows[bi], :] = (k * jnp.exp(-b)).astype(BF16)
            kh_scr[di, rows[bi], :] = (k * jnp.exp(tot - b)).astype(BF16)
            for ci in range(cpr):
                chunk = (i * GLA_PRE_BLOCKS + bi) * cpr + ci
                drow = pl.ds(pl.multiple_of(chunk * SUBLANES, SUBLANES), SUBLANES)
                d_scr[di, drow, :] = jnp.exp(tot[ci * C:ci * C + SUBLANES, :])
        return carry

    lax.fori_loop(0, S // (R * GLA_PRE_BLOCKS), pre, 0)

    G = GLA_GROUP

    def sweep(first_touch):
        def body(i, sts):
            steps = []
            for g in range(G):
                steps.append((0, i * G + g))
                steps.append((1, nc - 1 - (i * G + g)))
            rows = [pl.ds(pl.multiple_of(t * C, C), C) for _, t in steps]
            qt = [qt_scr[di, rows[n], :] for n, (di, _) in enumerate(steps)]
            v = [v_ref[r, :] for r in rows]
            raw = [_nt_dot(qt[n], kt_scr[di, rows[n], :]) for n, (di, _) in enumerate(steps)]
            upd = [_tn_dot(v[n], kh_scr[di, rows[n], :]) for n, (di, _) in enumerate(steps)]
            sc = [jnp.where(masks[di], raw[n], 0.0).astype(BF16) for n, (di, _) in enumerate(steps)]
            st = list(sts)
            for n, (di, t) in enumerate(steps):
                o = _nt_dot(qt[n], st[di].astype(BF16)) + jnp.dot(sc[n], v[n], preferred_element_type=F32)
                if first_touch:
                    acc_scr[rows[n], :] = o
                else:
                    acc_scr[rows[n], :] += o
                d = d_scr[di, pl.ds(pl.multiple_of(t * SUBLANES, SUBLANES), 1), :]
                st[di] = st[di] * d + upd[n]
            return tuple(st)
        return body

    zero = jnp.zeros((GLA_DV, GLA_DK), F32)
    half = nc // (2 * G)
    sts = lax.fori_loop(0, half, sweep(True), (zero, zero))
    lax.fori_loop(half, 2 * half, sweep(False), sts)

    def fin(i, carry):
        rows = pl.ds(pl.multiple_of(i * R, R), R)
        y = _rms_norm(acc_scr[rows, :], g_ref[...])
        o_ref[rows, :] = (y * grs_ref[rows, :].astype(F32)).astype(BF16)
        return carry

    lax.fori_loop(0, S // R, fin, 0, unroll=2)


def _gla_call(gqk, act, lr, w2f_pad, w2b_pad, bf, bb, gla_g, batch, seq, v_block0, g_block0):
    M = gqk.shape[0]
    H = GLA_HEADS
    nc = seq // GLA_CHUNK
    vmem = (2 * (2 * seq * GLA_DK * 4 + 2 * seq * GLA_DV * 2 + seq * LANES * 4 + seq * GLA_DV * 2)
            + 2 * 3 * seq * GLA_DK * 2 + 2 * nc * SUBLANES * GLA_DK * 4 + seq * GLA_DV * 4 + (8 << 20))
    return pl.pallas_call(
        _gla_kernel,
        grid=(batch, H),
        in_specs=[
            pl.BlockSpec((seq, GLA_DK), lambda b, h: (b, h)),
            pl.BlockSpec((seq, GLA_DK), lambda b, h: (b, H + h)),
            pl.BlockSpec((seq, GLA_DV), lambda b, h: (b, v_block0 + h)),
            pl.BlockSpec((seq, GLA_DV), lambda b, h: (b, g_block0 + h)),
            pl.BlockSpec((seq, LANES), lambda b, h: (b, 0)),
            pl.BlockSpec((LANES, GLA_DK), lambda b, h: (0, h)),
            pl.BlockSpec((LANES, GLA_DK), lambda b, h: (0, h)),
            pl.BlockSpec((1, GLA_DK), lambda b, h: (0, h)),
            pl.BlockSpec((1, GLA_DK), lambda b, h: (0, h)),
            pl.BlockSpec((1, GLA_DV), lambda b, h: (0, 0)),
        ],
        out_specs=pl.BlockSpec((seq, GLA_DV), lambda b, h: (b, h)),
        out_shape=jax.ShapeDtypeStruct((M, H * GLA_DV), BF16),
        scratch_shapes=[
            pltpu.VMEM((2, seq, GLA_DK), BF16),
            pltpu.VMEM((2, seq, GLA_DK), BF16),
            pltpu.VMEM((2, seq, GLA_DK), BF16),
            pltpu.VMEM((2, nc * SUBLANES, GLA_DK), F32),
            pltpu.VMEM((seq, GLA_DV), F32),
        ],
        compiler_params=_cparams(("arbitrary", "arbitrary"), vmem),
        name="gla",
    )(gqk, gqk, act, act, lr, w2f_pad, w2b_pad, bf, bb, gla_g)


def _post_kernel(od_ref, og_ref, ga_ref, gb_ref, x_ref, mod_ref, wbd_ref, wbg_ref, wo_ref, xo_ref):
    yd = jnp.dot(od_ref[...], wbd_ref[...], preferred_element_type=F32)
    yg = jnp.dot(og_ref[...], wbg_ref[...], preferred_element_type=F32)
    merged = (ga_ref[...].astype(F32) * yd + gb_ref[...].astype(F32) * yg).astype(BF16)
    out = jnp.dot(merged, wo_ref[...], preferred_element_type=F32)
    xo_ref[...] = x_ref[...] + mod_ref[0, 2:3, :] * out


def _post_call(od, og, act, x2, mod_l, wbd, wbg, wo, seq):
    M, D = x2.shape
    tm = POST_TM
    bpb = seq // tm
    kd, kg = od.shape[1], og.shape[1]
    const = dict(pipeline_mode=pl.Buffered(1))
    vmem = ((kd + kg + D) * D * 2 + 2 * tm * (kd + kg) * 2 + 2 * 2 * tm * D * 2 + 2 * 2 * tm * D * 4
            + 4 * tm * D * 4 + (4 << 20))
    return pl.pallas_call(
        _post_kernel,
        grid=(M // tm,),
        in_specs=[
            pl.BlockSpec((tm, kd), lambda i: (i, 0)),
            pl.BlockSpec((tm, kg), lambda i: (i, 0)),
            pl.BlockSpec((tm, D), lambda i: (i, 0)),
            pl.BlockSpec((tm, D), lambda i: (i, 1)),
            pl.BlockSpec((tm, D), lambda i: (i, 0)),
            pl.BlockSpec((1, 6, D), lambda i: (i // bpb, 0, 0)),
            pl.BlockSpec((kd, D), lambda i: (0, 0), **const),
            pl.BlockSpec((kg, D), lambda i: (0, 0), **const),
            pl.BlockSpec((D, D), lambda i: (0, 0), **const),
        ],
        out_specs=pl.BlockSpec((tm, D), lambda i: (i, 0)),
        out_shape=jax.ShapeDtypeStruct((M, D), F32),
        compiler_params=_cparams(("arbitrary",), vmem),
        name="mix_out",
    )(od, og, act, act, x2, mod_l, wbd, wbg, wo)


def _ffn_kernel(final_norm, bpb, nj, x_ref, xp_ref, xn_ref, mod_ref, g_ref, cw_ref, cb_ref, fg_ref,
                wg_hbm, wu_hbm, wd_hbm, o_ref, h_scr, g_scr, acc_scr, wg_buf, wu_buf, wd_buf, sem):
    i = pl.program_id(0)
    ni = pl.num_programs(0)
    tm, D = x_ref.shape
    tn = wg_buf.shape[2]
    hl = FFN_HALO

    def weight_copies(j, slot):
        cols = pl.ds(j * tn, tn)
        return (pltpu.make_async_copy(wg_hbm.at[:, cols], wg_buf.at[slot], sem.at[0, slot]),
                pltpu.make_async_copy(wu_hbm.at[:, cols], wu_buf.at[slot], sem.at[1, slot]),
                pltpu.make_async_copy(wd_hbm.at[cols, :], wd_buf.at[slot], sem.at[2, slot]))

    first_slot = (i * nj) % FFN_SLOTS

    def slot_of(j):
        return (first_slot + j) % FFN_SLOTS

    def start_block(j):
        if j < nj:
            for cp in weight_copies(j, slot_of(j)):
                cp.start()
        else:
            @pl.when(i + 1 < ni)
            def _():
                for cp in weight_copies(j - nj, slot_of(j)):
                    cp.start()

    def wait_block(j):
        if j < nj:
            for cp in weight_copies(j, slot_of(j)):
                cp.wait()
        else:
            @pl.when(i + 1 < ni)
            def _():
                for cp in weight_copies(j - nj, slot_of(j)):
                    cp.wait()

    @pl.when(i == 0)
    def _first_fetch():
        start_block(0)
        start_block(1)
        wait_block(0)

    def modulated(xv):
        return _rms_norm(xv, g_ref[...]) * (1.0 + mod_ref[0, 4:5, :]) + mod_ref[0, 3:4, :]

    in_seq_prev = (i % bpb) != 0
    in_seq_next = (i % bpb) != (bpb - 1)
    h_scr[0:hl, :] = jnp.where(in_seq_prev, modulated(xp_ref[...]), 0.0).astype(BF16)
    h_scr[hl:hl + tm, :] = modulated(x_ref[...]).astype(BF16)
    h_scr[hl + tm:hl + tm + hl, :] = jnp.where(in_seq_next, modulated(xn_ref[...]), 0.0).astype(BF16)
    acc_scr[...] = jnp.zeros_like(acc_scr)

    for j in range(nj):
        slot = slot_of(j)

        g_scr[...] = jnp.dot(h_scr[...], wg_buf[slot], preferred_element_type=F32)
        u = jnp.dot(h_scr[hl:hl + tm, :], wu_buf[slot], preferred_element_type=F32)
        cw = cw_ref[:, j * tn:(j + 1) * tn]
        gc = (cw[0:1] * g_scr[pl.ds(hl - 1, tm), :] + cw[1:2] * g_scr[pl.ds(hl, tm), :]
              + cw[2:3] * g_scr[pl.ds(hl + 1, tm), :] + cb_ref[:, j * tn:(j + 1) * tn])
        act = (gc * jax.nn.sigmoid(gc) * u).astype(BF16)

        wait_block(j + 1)
        start_block(j + 2)

        for c in range(D // FFN_NC):
            cols = slice(c * FFN_NC, (c + 1) * FFN_NC)
            acc_scr[:, cols] += jnp.dot(act, wd_buf[slot, :, cols], preferred_element_type=F32)

    xo = x_ref[...] + mod_ref[0, 5:6, :] * acc_scr[...]
    if final_norm:
        xo = _rms_norm(xo, fg_ref[...])
    o_ref[...] = xo


def _ffn_call(x2, mod_l, norm_g, wg, wu, conv_w, conv_b, wd, final_g, final_norm, seq):
    M, D = x2.shape
    Fdim = wg.shape[1]
    tm, tn, hl = FFN_TM, FFN_TN, FFN_HALO
    assert seq % tm == 0 and Fdim % tn == 0 and tm % hl == 0 and D % FFN_NC == 0
    nj = Fdim // tn
    assert nj >= 2
    bpb = seq // tm
    hpb = tm // hl
    nh = M // hl
    vmem = (2 * tm * D * 4 * 2 + 2 * 2 * hl * D * 4 + FFN_SLOTS * 3 * D * tn * 2 + (tm + 2 * hl) * D * 2
            + (tm + 2 * hl) * tn * 4 + tm * D * 4 + 6 * tm * tn * 4 + tm * D * 4 + 2 * 4 * Fdim * 4 * 2 + (4 << 20))
    hbm = pl.BlockSpec(memory_space=pl.ANY)
    return pl.pallas_call(
        functools.partial(_ffn_kernel, final_norm, bpb, nj),
        grid=(M // tm,),
        in_specs=[
            pl.BlockSpec((tm, D), lambda i: (i, 0)),
            pl.BlockSpec((hl, D), lambda i: (jnp.maximum(i * hpb - 1, 0), 0)),
            pl.BlockSpec((hl, D), lambda i: (jnp.minimum((i + 1) * hpb, nh - 1), 0)),
            pl.BlockSpec((1, 6, D), lambda i: (i // bpb, 0, 0)),
            pl.BlockSpec((1, D), lambda i: (0, 0)),
            pl.BlockSpec((CONV_WIDTH, Fdim), lambda i: (0, 0)),
            pl.BlockSpec((1, Fdim), lambda i: (0, 0)),
            pl.BlockSpec((1, D), lambda i: (0, 0)),
            hbm, hbm, hbm,
        ],
        out_specs=pl.BlockSpec((tm, D), lambda i: (i, 0)),
        out_shape=jax.ShapeDtypeStruct((M, D), F32),
        scratch_shapes=[
            pltpu.VMEM((tm + 2 * hl, D), BF16),
            pltpu.VMEM((tm + 2 * hl, tn), F32),
            pltpu.VMEM((tm, D), F32),
            pltpu.VMEM((FFN_SLOTS, D, tn), BF16),
            pltpu.VMEM((FFN_SLOTS, D, tn), BF16),
            pltpu.VMEM((FFN_SLOTS, tn, D), BF16),
            pltpu.SemaphoreType.DMA((3, FFN_SLOTS)),
        ],
        compiler_params=_cparams(("arbitrary",), vmem),
        name="conv_ffn",
    )(x2, x2, x2, mod_l, norm_g, conv_w, conv_b, final_g, wg, wu, wd)


def kernel(x, c, positions, w_ada, b_ada, norm_mix_g, w_in, lambda_q1, lambda_k1, lambda_q2, lambda_k2,
           diff_subln_g, gla_w2_fwd, gla_b_fwd, gla_w2_bwd, gla_b_bwd, gla_norm_g, w_branch_diff,
           w_branch_gla, w_out, norm_ffn_g, w_gate, w_up, conv_w, conv_b, w_down, final_norm_g):
    B, S, D = x.shape
    L = w_ada.shape[0]
    M = B * S
    x2 = x.reshape(M, D)
    pos2 = positions.reshape(M, 1).astype(jnp.int32)

    half = ROT_DIM // 2
    inv_freq = ROPE_THETA ** (-jnp.arange(0, ROT_DIM, 2, dtype=F32) / ROT_DIM)
    freq = jnp.zeros((1, LANES), F32).at[0, :half].set(inv_freq).at[0, half:ROT_DIM].set(inv_freq)
    sgn = jnp.zeros((1, LANES), F32).at[0, :half].set(-1.0).at[0, half:ROT_DIM].set(1.0)

    mod = _ada_call(c, w_ada, b_ada).reshape(L, B, 6, D)

    sizes = (DIFF_HEADS * 2 * DIFF_DQK, DIFF_HEADS * 2 * DIFF_DQK, DIFF_HEADS * DIFF_DV,
             GLA_HEADS * GLA_DK, GLA_HEADS * GLA_DK, GLA_HEADS * GLA_DV, GLA_HEADS * GLA_DV,
             GLA_GATE_RANK, GLA_GATE_RANK, D, D)
    offs = [0]
    for s_ in sizes:
        offs.append(offs[-1] + s_)
    dv_block0 = 2 * D // DIFF_DV
    gv_block0 = dv_block0 + DIFF_HEADS
    gr_block0 = gv_block0 + GLA_HEADS

    for l in range(L):
        wl = w_in[l]
        cols = [wl[:, offs[k]:offs[k + 1]] for k in range(len(sizes))]
        w_main = jnp.concatenate([cols[0], cols[1], cols[3], cols[4], cols[9], cols[10], cols[2], cols[5], cols[6]],
                                 axis=1).astype(BF16)
        w_lr = jnp.zeros((D, LANES), F32).at[:, :GLA_GATE_RANK].set(cols[7])
        w_lr = w_lr.at[:, GLA_GATE_RANK:2 * GLA_GATE_RANK].set(cols[8]).astype(BF16)
        nk = GLA_HEADS * GLA_DK
        w2f_pad = jnp.zeros((LANES, nk), F32).at[:GLA_GATE_RANK].set(gla_w2_fwd[l]).astype(BF16)
        w2b_pad = jnp.zeros((LANES, nk), F32).at[GLA_GATE_RANK:2 * GLA_GATE_RANK].set(gla_w2_bwd[l]).astype(BF16)
        lam_init = 0.8 - 0.6 * math.exp(-0.3 * l)
        lqk = jnp.stack([lambda_q1[l], lambda_k1[l], lambda_q2[l], lambda_k2[l]])

        qk, gqk, act, lr = _inproj_call(
            x2, mod[l], norm_mix_g[l].reshape(1, D), pos2, freq, sgn, w_main, w_lr, S)
        od = _attn_call(qk, act, lqk, diff_subln_g[l].reshape(1, DIFF_DV), lam_init, B, S, dv_block0)
        og = _gla_call(gqk, act, lr, w2f_pad, w2b_pad, gla_b_fwd[l].reshape(1, nk),
                       gla_b_bwd[l].reshape(1, nk), gla_norm_g[l].reshape(1, GLA_DV), B, S, gv_block0, gr_block0)
        x2 = _post_call(od, og, act, x2, mod[l], _cast_call(w_branch_diff, l),
                        _cast_call(w_branch_gla, l), _cast_call(w_out, l), S)
        x2 = _ffn_call(x2, mod[l], norm_ffn_g[l].reshape(1, D), _cast_call(w_gate, l), _cast_call(w_up, l),
                       conv_w[l], conv_b[l].reshape(1, -1), _cast_call(w_down, l),
                       final_norm_g.reshape(1, D), l == L - 1, S)
    return x2.reshape(B, S, D)
```

```python
import functools
import math

import jax
import jax.numpy as jnp
from jax import lax
from jax.experimental import pallas as pl
from jax.experimental.pallas import tpu as pltpu

F32 = jnp.float32
BF16 = jnp.bfloat16

DIFF_HEADS = 4
DIFF_DQK = 128
DIFF_DV = 2 * DIFF_DQK
ROT_DIM = DIFF_DQK // 4
ROPE_THETA = 500000.0
GLA_HEADS = 4
GLA_DK = 128
GLA_DV = 256
GLA_GATE_RANK = 16
GLA_TAU = 16.0
GLA_CHUNK = 64
CONV_WIDTH = 3
NORM_EPS = 1e-6
LOG2E = 1.4426950408889634

LANES = 128
SUBLANES = 8
BF16_SUBLANES = 16
VMEM_BUDGET_BYTES = 58 * 1024 * 1024

ADA_TN = 1536
CAST_BLOCK_BYTES = 4 * 1024 * 1024
ROT_TM = 2048
INPROJ_HEAD_TM = 512
INPROJ_ACT_TM = 1024
INPROJ_SUB = 512
ATTN_TQ = 512
ATTN_KC = 1024
ATTN_AHEAD = 1
GLA_ROWS = 256
GLA_PRE_BLOCKS = 4
GLA_GROUP = 8
POST_TM = 256
FFN_TM = 512
FFN_TN = 512
FFN_NC = 512
FFN_SLOTS = 3
FFN_HALO = BF16_SUBLANES


def _cparams(sem, vmem_bytes):
    return pltpu.CompilerParams(dimension_semantics=sem, vmem_limit_bytes=min(int(vmem_bytes), VMEM_BUDGET_BYTES))


def _nt_dot(a, b):
    return lax.dot_general(a, b, (((1,), (1,)), ((), ())), preferred_element_type=F32)


def _tn_dot(a, b):
    return lax.dot_general(a, b, (((0,), (0,)), ((), ())), preferred_element_type=F32)


def _rms_norm(x, g):
    return x * lax.rsqrt(jnp.mean(x * x, axis=-1, keepdims=True) + NORM_EPS) * g


def _ada_kernel(c_ref, w_ref, b_ref, o_ref):
    c = c_ref[...]
    ca = (c * jax.nn.sigmoid(c)).astype(BF16)
    o_ref[0] = jnp.dot(ca, w_ref[0].astype(BF16), preferred_element_type=F32) + b_ref[0]


def _ada_call(c, w_ada, b_ada):
    L, D, N = w_ada.shape
    B = c.shape[0]
    tn = ADA_TN
    vmem = 2 * (D * tn * 4) + D * tn * 2 + 4 * B * (D + 2 * tn) * 4 + (2 << 20)
    return pl.pallas_call(
        _ada_kernel,
        grid=(L, N // tn),
        in_specs=[
            pl.BlockSpec((B, D), lambda l, j: (0, 0)),
            pl.BlockSpec((1, D, tn), lambda l, j: (l, 0, j)),
            pl.BlockSpec((1, 1, tn), lambda l, j: (l, 0, j)),
        ],
        out_specs=pl.BlockSpec((1, B, tn), lambda l, j: (l, 0, j)),
        out_shape=jax.ShapeDtypeStruct((L, B, N), F32),
        compiler_params=_cparams(("arbitrary", "arbitrary"), vmem),
        name="ada_mod",
    )(c, w_ada, b_ada.reshape(L, 1, N))


def _cast_kernel(w_ref, o_ref):
    o_ref[...] = w_ref[0].astype(BF16)


def _cast_call(w3, l):
    _, R, C = w3.shape
    limit = max(BF16_SUBLANES, CAST_BLOCK_BYTES // (C * 4))
    tr = max(t for t in range(BF16_SUBLANES, min(R, limit) + 1, BF16_SUBLANES) if R % t == 0)
    return pl.pallas_call(
        _cast_kernel,
        grid=(R // tr,),
        in_specs=[pl.BlockSpec((1, tr, C), lambda i: (l, i, 0))],
        out_specs=pl.BlockSpec((tr, C), lambda i: (i, 0)),
        out_shape=jax.ShapeDtypeStruct((R, C), BF16),
        compiler_params=_cparams(("arbitrary",), 2 * tr * C * (4 + 2) + (2 << 20)),
        name="cast_bf16",
    )(w3)


def _rot_tables_kernel(pos_ref, freq_ref, sgn_ref, cos_ref, sin_ref):
    ang = pos_ref[...].astype(F32) * freq_ref[...]
    cos_ref[...] = jnp.cos(ang)
    sin_ref[...] = jnp.sin(ang) * sgn_ref[...]


def _rot_tables_call(pos2, freq, sgn):
    M = pos2.shape[0]
    tm = ROT_TM
    return pl.pallas_call(
        _rot_tables_kernel,
        grid=(M // tm,),
        in_specs=[
            pl.BlockSpec((tm, 1), lambda i: (i, 0)),
            pl.BlockSpec((1, LANES), lambda i: (0, 0)),
            pl.BlockSpec((1, LANES), lambda i: (0, 0)),
        ],
        out_specs=(pl.BlockSpec((tm, LANES), lambda i: (i, 0)), pl.BlockSpec((tm, LANES), lambda i: (i, 0))),
        out_shape=(jax.ShapeDtypeStruct((M, LANES), F32), jax.ShapeDtypeStruct((M, LANES), F32)),
        compiler_params=_cparams(("arbitrary",), 8 * tm * LANES * 4 + (2 << 20)),
        name="rot_tables",
    )(pos2, freq, sgn)


def _inproj_head_kernel(x_ref, mod_ref, g_ref, cos_ref, sin_ref, w_ref, wlr_ref,
                        qk_ref, gqk_ref, lr_ref, h_ref):
    tm = x_ref.shape[0]
    sub = INPROJ_SUB
    y = _rms_norm(x_ref[...], g_ref[...])
    h_ref[...] = (y * (1.0 + mod_ref[0, 1:2, :]) + mod_ref[0, 0:1, :]).astype(BF16)
    lr_ref[...] = jnp.dot(h_ref[...], wlr_ref[...], preferred_element_type=F32)

    lane = lax.broadcasted_iota(jnp.int32, (tm, LANES), 1)
    half = ROT_DIM // 2
    n_qk = qk_ref.shape[1] // sub
    n_q = n_qk // 2
    for c in range(w_ref.shape[1] // sub):
        acc = jnp.dot(h_ref[...], w_ref[:, c * sub:(c + 1) * sub], preferred_element_type=F32)
        if c < n_qk:
            qscale = (DIFF_DQK ** -0.5) * LOG2E if c < n_q else 1.0
            for gi in range(sub // LANES):
                t = acc[:, gi * LANES:(gi + 1) * LANES]
                partner = jnp.where(lane < half, pltpu.roll(t, LANES - half, 1), pltpu.roll(t, half, 1))
                rot = jnp.where(lane < ROT_DIM, t * cos_ref[...] + partner * sin_ref[...], t)
                lo = c * sub + gi * LANES
                qk_ref[:, lo:lo + LANES] = (rot * qscale).astype(BF16)
        else:
            lo = (c - n_qk) * sub
            gqk_ref[:, lo:lo + sub] = acc * (GLA_DK ** -0.5) if lo < GLA_HEADS * GLA_DK else acc


def _inproj_act_kernel(kinds, h_ref, w_ref, o_ref):
    sub = INPROJ_SUB
    for c, kind in enumerate(kinds):
        acc = jnp.dot(h_ref[...], w_ref[:, c * sub:(c + 1) * sub], preferred_element_type=F32)
        if kind == "sigmoid":
            acc = jax.nn.sigmoid(acc)
        elif kind == "silu":
            acc = acc * jax.nn.sigmoid(acc)
        o_ref[:, c * sub:(c + 1) * sub] = acc.astype(BF16)


def _inproj_head_call(x2, mod_l, norm_g, cos, sin, w_head, w_lr, seq):
    M, D = x2.shape
    tm = INPROJ_HEAD_TM
    bpb = seq // tm
    n_qk = 2 * DIFF_HEADS * 2 * DIFF_DQK
    n_gqk = 2 * GLA_HEADS * GLA_DK
    assert w_head.shape[1] == n_qk + n_gqk and seq % tm == 0
    const = dict(pipeline_mode=pl.Buffered(1))
    widths = (n_qk, n_gqk, LANES, D)
    dtypes = (BF16, F32, F32, BF16)
    out_bytes = sum(tm * w * jnp.dtype(dt).itemsize for w, dt in zip(widths, dtypes))
    vmem = (2 * tm * D * 4 + D * (n_qk + n_gqk + LANES) * 2 + 2 * out_bytes + 4 * tm * LANES * 4
            + 6 * tm * INPROJ_SUB * 4 + (4 << 20))
    return pl.pallas_call(
        _inproj_head_kernel,
        grid=(M // tm,),
        in_specs=[
            pl.BlockSpec((tm, D), lambda i: (i, 0)),
            pl.BlockSpec((1, 6, D), lambda i: (i // bpb, 0, 0)),
            pl.BlockSpec((1, D), lambda i: (0, 0)),
            pl.BlockSpec((tm, LANES), lambda i: (i, 0)),
            pl.BlockSpec((tm, LANES), lambda i: (i, 0)),
            pl.BlockSpec((D, n_qk + n_gqk), lambda i: (0, 0), **const),
            pl.BlockSpec((D, LANES), lambda i: (0, 0), **const),
        ],
        out_specs=tuple(pl.BlockSpec((tm, w), lambda i: (i, 0)) for w in widths),
        out_shape=tuple(jax.ShapeDtypeStruct((M, w), dt) for w, dt in zip(widths, dtypes)),
        compiler_params=_cparams(("arbitrary",), vmem),
        name="in_proj_head",
    )(x2, mod_l, norm_g, cos, sin, w_head, w_lr)


def _inproj_act_call(h, w, kinds, name):
    M, D = h.shape
    N = w.shape[1]
    tm = INPROJ_ACT_TM
    assert N == len(kinds) * INPROJ_SUB and M % tm == 0
    vmem = 2 * tm * D * 2 + D * N * 2 + 2 * tm * N * 2 + 6 * tm * INPROJ_SUB * 4 + (4 << 20)
    return pl.pallas_call(
        functools.partial(_inproj_act_kernel, kinds),
        grid=(M // tm,),
        in_specs=[
            pl.BlockSpec((tm, D), lambda i: (i, 0)),
            pl.BlockSpec((D, N), lambda i: (0, 0), pipeline_mode=pl.Buffered(1)),
        ],
        out_specs=pl.BlockSpec((tm, N), lambda i: (i, 0)),
        out_shape=jax.ShapeDtypeStruct((M, N), BF16),
        compiler_params=_cparams(("arbitrary",), vmem),
        name=name,
    )(h, w)


def _attn_kernel(lam_init, q_ref, k_ref, v_ref, lqk_ref, g_ref, o_ref):
    S = k_ref.shape[0]
    kc = ATTN_KC
    lqk = lqk_ref[...]
    lam = (jnp.exp(jnp.sum(lqk[0:1] * lqk[1:2], axis=-1, keepdims=True))
           - jnp.exp(jnp.sum(lqk[2:3] * lqk[3:4], axis=-1, keepdims=True)) + lam_init)

    items = [(c, mi) for c in range(S // kc) for mi in range(2)]

    def scores(item):
        c, mi = item
        sl = slice(mi * DIFF_DQK, (mi + 1) * DIFF_DQK)
        return _nt_dot(q_ref[:, sl], k_ref[c * kc:(c + 1) * kc, sl])

    m = [None, None]
    l = [None, None]
    acc = [None, None]
    ahead = ATTN_AHEAD
    queue = [scores(it) for it in items[:ahead]]
    for n, (c, mi) in enumerate(items):
        s = queue.pop(0)
        if n + ahead < len(items):
            queue.append(scores(items[n + ahead]))
        v = v_ref[c * kc:(c + 1) * kc, :]
        smax = jnp.max(s, axis=-1, keepdims=True)
        if c == 0:
            m[mi] = smax
            p = jnp.exp2((s - smax).astype(BF16))
            l[mi] = jnp.sum(p.astype(F32), axis=-1, keepdims=True)
            acc[mi] = jnp.dot(p, v, preferred_element_type=F32)
        else:
            m_new = jnp.maximum(m[mi], smax)
            alpha = jnp.exp2(m[mi] - m_new)
            p = jnp.exp2((s - m_new).astype(BF16))
            l[mi] = alpha * l[mi] + jnp.sum(p.astype(F32), axis=-1, keepdims=True)
            acc[mi] = alpha * acc[mi] + jnp.dot(p, v, preferred_element_type=F32)
            m[mi] = m_new

    o = acc[0] / l[0] - lam * (acc[1] / l[1])
    o_ref[...] = (_rms_norm(o, g_ref[...]) * (1.0 - lam_init)).astype(BF16)


def _attn_call(qk, act, lqk, subln_g, lam_init, batch, seq, v_block0):
    M = qk.shape[0]
    tq = ATTN_TQ
    nq = seq // tq
    H = DIFF_HEADS
    w = 2 * DIFF_DQK
    vmem = (2 * tq * w * 2 * 2 + 2 * 2 * seq * w * 2 + 6 * tq * ATTN_KC * 4 + 6 * tq * DIFF_DV * 4 + (4 << 20))
    return pl.pallas_call(
        functools.partial(_attn_kernel, lam_init),
        grid=(batch, H, nq),
        in_specs=[
            pl.BlockSpec((tq, w), lambda b, h, i: (b * nq + i, h)),
            pl.BlockSpec((seq, w), lambda b, h, i: (b, H + h)),
            pl.BlockSpec((seq, DIFF_DV), lambda b, h, i: (b, v_block0 + h)),
            pl.BlockSpec((4, DIFF_DQK), lambda b, h, i: (0, 0)),
            pl.BlockSpec((1, DIFF_DV), lambda b, h, i: (0, 0)),
        ],
        out_specs=pl.BlockSpec((tq, DIFF_DV), lambda b, h, i: (b * nq + i, h)),
        out_shape=jax.ShapeDtypeStruct((M, H * DIFF_DV), BF16),
        compiler_params=_cparams(("arbitrary", "arbitrary", "arbitrary"), vmem),
        name="diff_attn",
    )(qk, qk, act, lqk, subln_g)


def _gla_kernel(q_ref, k_ref, v_ref, grs_ref, lr_ref, w2f_ref, w2b_ref, bf_ref, bb_ref, g_ref, o_ref,
                qt_scr, kt_scr, kh_scr, d_scr, acc_scr):
    S = q_ref.shape[0]
    C = GLA_CHUNK
    R = GLA_ROWS
    nc = S // C
    cpr = R // C
    shift = int(math.log2(C))

    rr = lax.broadcasted_iota(jnp.int32, (R, R), 0)
    cc = lax.broadcasted_iota(jnp.int32, (R, R), 1)
    same = (rr >> shift) == (cc >> shift)
    ones_bd = jnp.where(same, 1.0, 0.0)
    stack = tuple(
        jnp.concatenate([jnp.where(jnp.logical_and(same, keep), 1.0, 0.0), ones_bd], axis=0).astype(BF16)
        for keep in (cc <= rr, cc >= rr))
    r1 = lax.broadcasted_iota(jnp.int32, (C, C), 0)
    c1 = lax.broadcasted_iota(jnp.int32, (C, C), 1)
    masks = (c1 <= r1, c1 >= r1)
    w2 = (w2f_ref[...], w2b_ref[...])
    bias = (bf_ref[...], bb_ref[...])
    dk = GLA_DK

    def pre(i, carry):
        items = [(bi, di) for bi in range(GLA_PRE_BLOCKS) for di in range(2)]
        rows = [pl.ds(pl.multiple_of((i * GLA_PRE_BLOCKS + bi) * R, R), R) for bi in range(GLA_PRE_BLOCKS)]
        lr = [lr_ref[r, :].astype(BF16) for r in rows]
        xg = [jnp.dot(lr[bi], w2[di], preferred_element_type=F32) + bias[di] for bi, di in items]
        hilo = []
        for x in xg:
            log_a = (jnp.minimum(x, 0.0) - jnp.log(1.0 + jnp.exp(-jnp.abs(x)))) * (1.0 / GLA_TAU)
            hi = log_a.astype(BF16)
            lo = (log_a - hi.astype(F32)).astype(BF16)
            hilo.append(jnp.concatenate([hi, lo], axis=1))
        sums = [jnp.dot(stack[di], hilo[n], preferred_element_type=F32) for n, (bi, di) in enumerate(items)]
        for n, (bi, di) in enumerate(items):
            sm = sums[n][:, :dk] + sums[n][:, dk:]
            b = sm[:R]
            tot = sm[R:]
            q = q_ref[rows[bi], :]
            k = k_ref[rows[bi], :]
            qt_scr[di, rows[bi], :] = (q * jnp.exp(b)).astype(BF16)
            kt_scr[di, rows[bi], :] = (k * jnp.exp(-b)).astype(BF16)
            kh_scr[di, rows[bi], :] = (k * jnp.exp(tot - b)).astype(BF16)
            for ci in range(cpr):
                chunk = (i * GLA_PRE_BLOCKS + bi) * cpr + ci
                drow = pl.ds(pl.multiple_of(chunk * SUBLANES, SUBLANES), SUBLANES)
                d_scr[di, drow, :] = jnp.exp(tot[ci * C:ci * C + SUBLANES, :])
        return carry

    lax.fori_loop(0, S // (R * GLA_PRE_BLOCKS), pre, 0)

    G = GLA_GROUP

    def sweep(first_touch):
        def body(i, sts):
            steps = []
            for g in range(G):
                steps.append((0, i * G + g))
                steps.append((1, nc - 1 - (i * G + g)))
            rows = [pl.ds(pl.multiple_of(t * C, C), C) for _, t in steps]
            qt = [qt_scr[di, rows[n], :] for n, (di, _) in enumerate(steps)]
            v = [v_ref[r, :] for r in rows]
            raw = [_nt_dot(qt[n], kt_scr[di, rows[n], :]) for n, (di, _) in enumerate(steps)]
            upd = [_tn_dot(v[n], kh_scr[di, rows[n], :]) for n, (di, _) in enumerate(steps)]
            sc = [jnp.where(masks[di], raw[n], 0.0).astype(BF16) for n, (di, _) in enumerate(steps)]
            st = list(sts)
            for n, (di, t) in enumerate(steps):
                o = _nt_dot(qt[n], st[di].astype(BF16)) + jnp.dot(sc[n], v[n], preferred_element_type=F32)
                if first_touch:
                    acc_scr[rows[n], :] = o
                else:
                    acc_scr[rows[n], :] += o
                d = d_scr[di, pl.ds(pl.multiple_of(t * SUBLANES, SUBLANES), 1), :]
                st[di] = st[di] * d + upd[n]
            return tuple(st)
        return body

    zero = jnp.zeros((GLA_DV, GLA_DK), F32)
    half = nc // (2 * G)
    sts = lax.fori_loop(0, half, sweep(True), (zero, zero))
    lax.fori_loop(half, 2 * half, sweep(False), sts)

    def fin(i, carry):
        rows = pl.ds(pl.multiple_of(i * R, R), R)
        y = _rms_norm(acc_scr[rows, :], g_ref[...])
        o_ref[rows, :] = (y * grs_ref[rows, :].astype(F32)).astype(BF16)
        return carry

    lax.fori_loop(0, S // R, fin, 0, unroll=2)


def _gla_call(gqk, act, lr, w2f_pad, w2b_pad, bf, bb, gla_g, batch, seq, v_block0, g_block0):
    M = gqk.shape[0]
    H = GLA_HEADS
    nc = seq // GLA_CHUNK
    vmem = (2 * (2 * seq * GLA_DK * 4 + 2 * seq * GLA_DV * 2 + seq * LANES * 4 + seq * GLA_DV * 2)
            + 2 * 3 * seq * GLA_DK * 2 + 2 * nc * SUBLANES * GLA_DK * 4 + seq * GLA_DV * 4 + (8 << 20))
    return pl.pallas_call(
        _gla_kernel,
        grid=(batch, H),
        in_specs=[
            pl.BlockSpec((seq, GLA_DK), lambda b, h: (b, h)),
            pl.BlockSpec((seq, GLA_DK), lambda b, h: (b, H + h)),
            pl.BlockSpec((seq, GLA_DV), lambda b, h: (b, v_block0 + h)),
            pl.BlockSpec((seq, GLA_DV), lambda b, h: (b, g_block0 + h)),
            pl.BlockSpec((seq, LANES), lambda b, h: (b, 0)),
            pl.BlockSpec((LANES, GLA_DK), lambda b, h: (0, h)),
            pl.BlockSpec((LANES, GLA_DK), lambda b, h: (0, h)),
            pl.BlockSpec((1, GLA_DK), lambda b, h: (0, h)),
            pl.BlockSpec((1, GLA_DK), lambda b, h: (0, h)),
            pl.BlockSpec((1, GLA_DV), lambda b, h: (0, 0)),
        ],
        out_specs=pl.BlockSpec((seq, GLA_DV), lambda b, h: (b, h)),
        out_shape=jax.ShapeDtypeStruct((M, H * GLA_DV), BF16),
        scratch_shapes=[
            pltpu.VMEM((2, seq, GLA_DK), BF16),
            pltpu.VMEM((2, seq, GLA_DK), BF16),
            pltpu.VMEM((2, seq, GLA_DK), BF16),
            pltpu.VMEM((2, nc * SUBLANES, GLA_DK), F32),
            pltpu.VMEM((seq, GLA_DV), F32),
        ],
        compiler_params=_cparams(("arbitrary", "arbitrary"), vmem),
        name="gla",
    )(gqk, gqk, act, act, lr, w2f_pad, w2b_pad, bf, bb, gla_g)


def _post_kernel(od_ref, og_ref, ga_ref, gb_ref, x_ref, mod_ref, wbd_ref, wbg_ref, wo_ref, xo_ref):
    yd = jnp.dot(od_ref[...], wbd_ref[...], preferred_element_type=F32)
    yg = jnp.dot(og_ref[...], wbg_ref[...], preferred_element_type=F32)
    merged = (ga_ref[...].astype(F32) * yd + gb_ref[...].astype(F32) * yg).astype(BF16)
    out = jnp.dot(merged, wo_ref[...], preferred_element_type=F32)
    xo_ref[...] = x_ref[...] + mod_ref[0, 2:3, :] * out


def _post_call(od, og, act, x2, mod_l, wbd, wbg, wo, seq):
    M, D = x2.shape
    tm = POST_TM
    bpb = seq // tm
    kd, kg = od.shape[1], og.shape[1]
    const = dict(pipeline_mode=pl.Buffered(1))
    vmem = ((kd + kg + D) * D * 2 + 2 * tm * (kd + kg) * 2 + 2 * 2 * tm * D * 2 + 2 * 2 * tm * D * 4
            + 4 * tm * D * 4 + (4 << 20))
    return pl.pallas_call(
        _post_kernel,
        grid=(M // tm,),
        in_specs=[
            pl.BlockSpec((tm, kd), lambda i: (i, 0)),
            pl.BlockSpec((tm, kg), lambda i: (i, 0)),
            pl.BlockSpec((tm, D), lambda i: (i, 0)),
            pl.BlockSpec((tm, D), lambda i: (i, 1)),
            pl.BlockSpec((tm, D), lambda i: (i, 0)),
            pl.BlockSpec((1, 6, D), lambda i: (i // bpb, 0, 0)),
            pl.BlockSpec((kd, D), lambda i: (0, 0), **const),
            pl.BlockSpec((kg, D), lambda i: (0, 0), **const),
            pl.BlockSpec((D, D), lambda i: (0, 0), **const),
        ],
        out_specs=pl.BlockSpec((tm, D), lambda i: (i, 0)),
        out_shape=jax.ShapeDtypeStruct((M, D), F32),
        compiler_params=_cparams(("arbitrary",), vmem),
        name="mix_out",
    )(od, og, act, act, x2, mod_l, wbd, wbg, wo)


def _ffn_kernel(final_norm, bpb, nj, x_ref, xp_ref, xn_ref, mod_ref, g_ref, cw_ref, cb_ref, fg_ref,
                wg_hbm, wu_hbm, wd_hbm, o_ref, h_scr, g_scr, acc_scr, wg_buf, wu_buf, wd_buf, sem):
    i = pl.program_id(0)
    ni = pl.num_programs(0)
    tm, D = x_ref.shape
    tn = wg_buf.shape[2]
    hl = FFN_HALO

    def weight_copies(j, slot):
        cols = pl.ds(j * tn, tn)
        return (pltpu.make_async_copy(wg_hbm.at[:, cols], wg_buf.at[slot], sem.at[0, slot]),
                pltpu.make_async_copy(wu_hbm.at[:, cols], wu_buf.at[slot], sem.at[1, slot]),
                pltpu.make_async_copy(wd_hbm.at[cols, :], wd_buf.at[slot], sem.at[2, slot]))

    first_slot = (i * nj) % FFN_SLOTS

    def slot_of(j):
        return (first_slot + j) % FFN_SLOTS

    def start_block(j):
        if j < nj:
            for cp in weight_copies(j, slot_of(j)):
                cp.start()
        else:
            @pl.when(i + 1 < ni)
            def _():
                for cp in weight_copies(j - nj, slot_of(j)):
                    cp.start()

    def wait_block(j):
        if j < nj:
            for cp in weight_copies(j, slot_of(j)):
                cp.wait()
        else:
            @pl.when(i + 1 < ni)
            def _():
                for cp in weight_copies(j - nj, slot_of(j)):
                    cp.wait()

    @pl.when(i == 0)
    def _first_fetch():
        start_block(0)
        start_block(1)
        wait_block(0)

    def modulated(xv):
        return _rms_norm(xv, g_ref[...]) * (1.0 + mod_ref[0, 4:5, :]) + mod_ref[0, 3:4, :]

    in_seq_prev = (i % bpb) != 0
    in_seq_next = (i % bpb) != (bpb - 1)
    h_scr[0:hl, :] = jnp.where(in_seq_prev, modulated(xp_ref[...]), 0.0).astype(BF16)
    h_scr[hl:hl + tm, :] = modulated(x_ref[...]).astype(BF16)
    h_scr[hl + tm:hl + tm + hl, :] = jnp.where(in_seq_next, modulated(xn_ref[...]), 0.0).astype(BF16)
    acc_scr[...] = jnp.zeros_like(acc_scr)

    for j in range(nj):
        slot = slot_of(j)

        g_scr[...] = jnp.dot(h_scr[...], wg_buf[slot], preferred_element_type=F32)
        u = jnp.dot(h_scr[hl:hl + tm, :], wu_buf[slot], preferred_element_type=F32)
        cw = cw_ref[:, j * tn:(j + 1) * tn]
        gc = (cw[0:1] * g_scr[pl.ds(hl - 1, tm), :] + cw[1:2] * g_scr[pl.ds(hl, tm), :]
              + cw[2:3] * g_scr[pl.ds(hl + 1, tm), :] + cb_ref[:, j * tn:(j + 1) * tn])
        act = (gc * jax.nn.sigmoid(gc) * u).astype(BF16)

        wait_block(j + 1)
        start_block(j + 2)

        for c in range(D // FFN_NC):
            cols = slice(c * FFN_NC, (c + 1) * FFN_NC)
            acc_scr[:, cols] += jnp.dot(act, wd_buf[slot, :, cols], preferred_element_type=F32)

    xo = x_ref[...] + mod_ref[0, 5:6, :] * acc_scr[...]
    if final_norm:
        xo = _rms_norm(xo, fg_ref[...])
    o_ref[...] = xo


def _ffn_call(x2, mod_l, norm_g, wg, wu, conv_w, conv_b, wd, final_g, final_norm, seq):
    M, D = x2.shape
    Fdim = wg.shape[1]
    tm, tn, hl = FFN_TM, FFN_TN, FFN_HALO
    assert seq % tm == 0 and Fdim % tn == 0 and tm % hl == 0 and D % FFN_NC == 0
    nj = Fdim // tn
    assert nj >= 2
    bpb = seq // tm
    hpb = tm // hl
    nh = M // hl
    vmem = (2 * tm * D * 4 * 2 + 2 * 2 * hl * D * 4 + FFN_SLOTS * 3 * D * tn * 2 + (tm + 2 * hl) * D * 2
            + (tm + 2 * hl) * tn * 4 + tm * D * 4 + 6 * tm * tn * 4 + tm * D * 4 + 2 * 4 * Fdim * 4 * 2 + (4 << 20))
    hbm = pl.BlockSpec(memory_space=pl.ANY)
    return pl.pallas_call(
        functools.partial(_ffn_kernel, final_norm, bpb, nj),
        grid=(M // tm,),
        in_specs=[
            pl.BlockSpec((tm, D), lambda i: (i, 0)),
            pl.BlockSpec((hl, D), lambda i: (jnp.maximum(i * hpb - 1, 0), 0)),
            pl.BlockSpec((hl, D), lambda i: (jnp.minimum((i + 1) * hpb, nh - 1), 0)),
            pl.BlockSpec((1, 6, D), lambda i: (i // bpb, 0, 0)),
            pl.BlockSpec((1, D), lambda i: (0, 0)),
            pl.BlockSpec((CONV_WIDTH, Fdim), lambda i: (0, 0)),
            pl.BlockSpec((1, Fdim), lambda i: (0, 0)),
            pl.BlockSpec((1, D), lambda i: (0, 0)),
            hbm, hbm, hbm,
        ],
        out_specs=pl.BlockSpec((tm, D), lambda i: (i, 0)),
        out_shape=jax.ShapeDtypeStruct((M, D), F32),
        scratch_shapes=[
            pltpu.VMEM((tm + 2 * hl, D), BF16),
            pltpu.VMEM((tm + 2 * hl, tn), F32),
            pltpu.VMEM((tm, D), F32),
            pltpu.VMEM((FFN_SLOTS, D, tn), BF16),
            pltpu.VMEM((FFN_SLOTS, D, tn), BF16),
            pltpu.VMEM((FFN_SLOTS, tn, D), BF16),
            pltpu.SemaphoreType.DMA((3, FFN_SLOTS)),
        ],
        compiler_params=_cparams(("arbitrary",), vmem),
        name="conv_ffn",
    )(x2, x2, x2, mod_l, norm_g, conv_w, conv_b, final_g, wg, wu, wd)


def kernel(x, c, positions, w_ada, b_ada, norm_mix_g, w_in, lambda_q1, lambda_k1, lambda_q2, lambda_k2,
           diff_subln_g, gla_w2_fwd, gla_b_fwd, gla_w2_bwd, gla_b_bwd, gla_norm_g, w_branch_diff,
           w_branch_gla, w_out, norm_ffn_g, w_gate, w_up, conv_w, conv_b, w_down, final_norm_g):
    B, S, D = x.shape
    L = w_ada.shape[0]
    M = B * S
    x2 = x.reshape(M, D)
    pos2 = positions.reshape(M, 1).astype(jnp.int32)

    half = ROT_DIM // 2
    inv_freq = ROPE_THETA ** (-jnp.arange(0, ROT_DIM, 2, dtype=F32) / ROT_DIM)
    freq = jnp.zeros((1, LANES), F32).at[0, :half].set(inv_freq).at[0, half:ROT_DIM].set(inv_freq)
    sgn = jnp.zeros((1, LANES), F32).at[0, :half].set(-1.0).at[0, half:ROT_DIM].set(1.0)

    cos, sin = _rot_tables_call(pos2, freq, sgn)
    mod = _ada_call(c, w_ada, b_ada).reshape(L, B, 6, D)

    sizes = (DIFF_HEADS * 2 * DIFF_DQK, DIFF_HEADS * 2 * DIFF_DQK, DIFF_HEADS * DIFF_DV,
             GLA_HEADS * GLA_DK, GLA_HEADS * GLA_DK, GLA_HEADS * GLA_DV, GLA_HEADS * GLA_DV,
             GLA_GATE_RANK, GLA_GATE_RANK, D, D)
    offs = [0]
    for s_ in sizes:
        offs.append(offs[-1] + s_)
    sub_per = lambda n: n // INPROJ_SUB
    gate_kinds = ("sigmoid",) * sub_per(2 * D)
    val_kinds = (("plain",) * sub_per(DIFF_HEADS * DIFF_DV + GLA_HEADS * GLA_DV)
                 + ("silu",) * sub_per(GLA_HEADS * GLA_DV))
    gv_block0 = DIFF_HEADS
    gr_block0 = gv_block0 + GLA_HEADS

    for l in range(L):
        wl = w_in[l]
        cols = [wl[:, offs[k]:offs[k + 1]] for k in range(len(sizes))]
        w_head = jnp.concatenate([cols[0], cols[1], cols[3], cols[4]], axis=1).astype(BF16)
        w_gate_ab = jnp.concatenate([cols[9], cols[10]], axis=1).astype(BF16)
        w_val = jnp.concatenate([cols[2], cols[5], cols[6]], axis=1).astype(BF16)
        w_lr = jnp.zeros((D, LANES), F32).at[:, :GLA_GATE_RANK].set(cols[7])
        w_lr = w_lr.at[:, GLA_GATE_RANK:2 * GLA_GATE_RANK].set(cols[8]).astype(BF16)
        nk = GLA_HEADS * GLA_DK
        w2f_pad = jnp.zeros((LANES, nk), F32).at[:GLA_GATE_RANK].set(gla_w2_fwd[l]).astype(BF16)
        w2b_pad = jnp.zeros((LANES, nk), F32).at[GLA_GATE_RANK:2 * GLA_GATE_RANK].set(gla_w2_bwd[l]).astype(BF16)
        lam_init = 0.8 - 0.6 * math.exp(-0.3 * l)
        lqk = jnp.stack([lambda_q1[l], lambda_k1[l], lambda_q2[l], lambda_k2[l]])

        qk, gqk, lr, h = _inproj_head_call(
            x2, mod[l], norm_mix_g[l].reshape(1, D), cos, sin, w_head, w_lr, S)
        gates = _inproj_act_call(h, w_gate_ab, gate_kinds, "in_proj_gate")
        vals = _inproj_act_call(h, w_val, val_kinds, "in_proj_val")
        od = _attn_call(qk, vals, lqk, diff_subln_g[l].reshape(1, DIFF_DV), lam_init, B, S, 0)
        og = _gla_call(gqk, vals, lr, w2f_pad, w2b_pad, gla_b_fwd[l].reshape(1, nk),
                       gla_b_bwd[l].reshape(1, nk), gla_norm_g[l].reshape(1, GLA_DV), B, S, gv_block0, gr_block0)
        x2 = _post_call(od, og, gates, x2, mod[l], _cast_call(w_branch_diff, l),
                        _cast_call(w_branch_gla, l), _cast_call(w_out, l), S)
        x2 = _ffn_call(x2, mod[l], norm_ffn_g[l].reshape(1, D), _cast_call(w_gate, l), _cast_call(w_up, l),
                       conv_w[l], conv_b[l].reshape(1, -1), _cast_call(w_down, l),
                       final_norm_g.reshape(1, D), l == L - 1, S)
    return x2.reshape(B, S, D)
```
